```python
import jax
import jax.numpy as jnp
from jax import lax
import numpy as np

D_MODEL = 2048
BATCH = 1
SEQ = 8192
DEPTH = 2
DEC_BATCH = 8
DEC_SEQ = 32
PAST_LEN = 1024

CHUNK = 64
N_MIXERS = 2
N_DSA_LAYERS = (DEPTH + 1) // 2
N_SWA_LAYERS = DEPTH // 2
ROPE_THETA = 500000.0
NORM_EPS = 1e-6
FFN_DIM = 5632

A_HEADS = 16
A_KV_HEADS = 4
A_HEAD_DIM = 128
A_ROT = A_HEAD_DIM // 4
IDX_HEADS = 16
IDX_DIM = 128
IDX_ROT = IDX_DIM // 4
TOPK_MAX = 256
A_QBLOCK = 128
A_SPLITS = (A_HEADS * A_HEAD_DIM, A_KV_HEADS * A_HEAD_DIM, A_KV_HEADS * A_HEAD_DIM,
            IDX_HEADS * IDX_DIM, IDX_DIM, IDX_HEADS)
A_IN = sum(A_SPLITS)

B_HEADS = 32
B_KV_HEADS = 4
B_GROUP = B_HEADS // B_KV_HEADS
B_HEAD_DIM = 64
B_ROT = B_HEAD_DIM // 4
WINDOW = 128
W_CHUNKS = WINDOW // CHUNK
B_SPLITS = (B_HEADS * B_HEAD_DIM, B_KV_HEADS * B_HEAD_DIM, B_KV_HEADS * B_HEAD_DIM)
B_IN = sum(B_SPLITS)

kernel_name = 'hybrid_dsa_swa_macaron_stream_step'


def rms_norm(x, g):
    xf = x.astype(jnp.float32)
    y = xf * lax.rsqrt(jnp.mean(xf * xf, axis=-1, keepdims=True) + NORM_EPS)
    return (y * g.astype(jnp.float32)).astype(x.dtype)


def layer_norm(x, g, b):
    xf = x.astype(jnp.float32)
    mu = jnp.mean(xf, axis=-1, keepdims=True)
    var = jnp.mean(jnp.square(xf - mu), axis=-1, keepdims=True)
    y = (xf - mu) * lax.rsqrt(var + NORM_EPS)
    return (y * g.astype(jnp.float32) + b.astype(jnp.float32)).astype(x.dtype)


def partial_rope(x, pos, rot):
    half = rot // 2
    inv_freq = jnp.float32(ROPE_THETA) ** (-jnp.arange(half, dtype=jnp.float32) / half)
    ang = pos.astype(jnp.float32)[:, None] * inv_freq[None, :]
    cos = jnp.cos(ang)[:, None, :]
    sin = jnp.sin(ang)[:, None, :]
    xr = x[..., :rot].astype(jnp.float32)
    x1, x2 = xr[..., :half], xr[..., half:]
    xr = jnp.concatenate([x1 * cos - x2 * sin, x2 * cos + x1 * sin], axis=-1)
    return jnp.concatenate([xr.astype(x.dtype), x[..., rot:]], axis=-1)


def split_cols(p, sizes):
    return jnp.split(p, np.cumsum(sizes)[:-1].tolist(), axis=-1)


def half_ffn(x, g, w_gate, w_up, w_down):
    h = rms_norm(x, g)
    return x + 0.5 * ((jax.nn.silu(h @ w_gate) * (h @ w_up)) @ w_down)


def dsa_project(h, w_in, kln_g, kln_b, pos):
    b, t, _ = h.shape
    q, k, v, qi, ki, wh = split_cols(h @ w_in, A_SPLITS)
    q = partial_rope(q.reshape(b, t, A_HEADS, A_HEAD_DIM), pos, A_ROT)
    k = partial_rope(k.reshape(b, t, A_KV_HEADS, A_HEAD_DIM), pos, A_ROT)
    v = v.reshape(b, t, A_KV_HEADS, A_HEAD_DIM)
    qi = partial_rope(qi.reshape(b, t, IDX_HEADS, IDX_DIM), pos, IDX_ROT)
    ki = partial_rope(layer_norm(ki, kln_g, kln_b)[:, :, None, :], pos, IDX_ROT)[:, :, 0, :]
    wh = wh * IDX_HEADS ** -0.5
    return q, k, v, qi, ki, wh


def dsa_attend(q, qi, wh, qpos, k, v, ki, kpos, topk):
    b, t = q.shape[:2]
    logits = jnp.einsum('bthd,bsd->bths', qi, ki, preferred_element_type=jnp.float32) * IDX_DIM ** -0.5
    score = jnp.einsum('bth,bths->bts', wh.astype(jnp.float32), jax.nn.relu(logits))
    admissible = (kpos[None, :] // CHUNK) <= (qpos[:, None] // CHUNK)
    score = jnp.where(admissible[None], score, -jnp.inf)
    top_val, top_idx = lax.top_k(score, topk)
    valid = jnp.isfinite(top_val)
    bidx = jnp.arange(b)[:, None, None]
    k_sel = k[bidx, top_idx]
    v_sel = v[bidx, top_idx]
    qg = q.reshape(b, t, A_KV_HEADS, A_HEADS // A_KV_HEADS, A_HEAD_DIM)
    s = jnp.einsum('btkgd,btskd->btkgs', qg, k_sel, preferred_element_type=jnp.float32) * A_HEAD_DIM ** -0.5
    s = jnp.where(valid[:, :, None, None, :], s, -jnp.inf)
    p = jax.nn.softmax(s, axis=-1).astype(v.dtype)
    o = jnp.einsum('btkgs,btskd->btkgd', p, v_sel)
    return o.reshape(b, t, A_HEADS * A_HEAD_DIM)


def dsa_prompt(h, w_in, kln_g, kln_b, w_out):
    b, s_len, _ = h.shape
    pos = jnp.arange(s_len)
    q, k, v, qi, ki, wh = dsa_project(h, w_in, kln_g, kln_b, pos)
    topk = min(TOPK_MAX, s_len // 4)
    nb = s_len // A_QBLOCK

    def to_blocks(a):
        return jnp.moveaxis(a.reshape((b, nb, A_QBLOCK) + a.shape[2:]), 1, 0)

    def one_block(args):
        j, qb, qib, whb = args
        qpos = j * A_QBLOCK + jnp.arange(A_QBLOCK)
        return dsa_attend(qb, qib, whb, qpos, k, v, ki, pos, topk)

    o = lax.map(one_block, (jnp.arange(nb), to_blocks(q), to_blocks(qi), to_blocks(wh)))
    o = jnp.moveaxis(o, 0, 1).reshape(b, s_len, A_HEADS * A_HEAD_DIM)
    return o @ w_out, k, v, ki


def dsa_sample(h, cache_k, cache_v, cache_ki, w_in, kln_g, kln_b, w_out):
    b, t, _ = h.shape
    past = cache_k.shape[1]
    qpos = past + jnp.arange(t)
    q, k, v, qi, ki, wh = dsa_project(h, w_in, kln_g, kln_b, qpos)
    k_all = jnp.concatenate([cache_k, k], axis=1)
    v_all = jnp.concatenate([cache_v, v], axis=1)
    ki_all = jnp.concatenate([cache_ki, ki], axis=1)
    topk = min(TOPK_MAX, (past + t) // 4)
    o = dsa_attend(q, qi, wh, qpos, k_all, v_all, ki_all, jnp.arange(past + t), topk)
    return o @ w_out, k, v, ki


def swa_project(h, w_in, b_in, pos):
    b, t, _ = h.shape
    q, k, v = split_cols(h @ w_in + b_in, B_SPLITS)
    q = partial_rope(q.reshape(b, t, B_HEADS, B_HEAD_DIM), pos, B_ROT)
    k = partial_rope(k.reshape(b, t, B_KV_HEADS, B_HEAD_DIM), pos, B_ROT)
    v = v.reshape(b, t, B_KV_HEADS, B_HEAD_DIM)
    return q, k, v


def sink_attend(q, k, v, mask, sinks):
    s = jnp.einsum('bcqkgd,bcskd->bckgqs', q, k, preferred_element_type=jnp.float32) * B_HEAD_DIM ** -0.5
    s = jnp.where(mask[None, :, None, None], s, -jnp.inf)
    sink = sinks.astype(jnp.float32).reshape(1, 1, B_KV_HEADS, B_GROUP, 1, 1)
    m = jnp.maximum(jnp.max(s, axis=-1, keepdims=True), sink)
    p = jnp.exp(s - m)
    p = p / (jnp.sum(p, axis=-1, keepdims=True) + jnp.exp(sink - m))
    return jnp.einsum('bckgqs,bcskd->bcqkgd', p.astype(v.dtype), v)


def swa_prompt(h, w_in, b_in, sinks, w_out):
    b, s_len, _ = h.shape
    nc = s_len // CHUNK
    q, k, v = swa_project(h, w_in, b_in, jnp.arange(s_len))
    qc = q.reshape(b, nc, CHUNK, B_KV_HEADS, B_GROUP, B_HEAD_DIM)

    def band(a):
        ap = jnp.pad(a.reshape(b, nc, CHUNK, B_KV_HEADS, B_HEAD_DIM), ((0, 0), (W_CHUNKS, 0), (0, 0), (0, 0), (0, 0)))
        return jnp.concatenate([ap[:, j:j + nc] for j in range(W_CHUNKS + 1)], axis=2)

    tk = (W_CHUNKS + 1) * CHUNK
    kchunk = jnp.arange(nc)[:, None] - W_CHUNKS + jnp.arange(tk)[None, :] // CHUNK
    mask = jnp.broadcast_to((kchunk >= 0)[:, None, :], (nc, CHUNK, tk))
    o = sink_attend(qc, band(k), band(v), mask, sinks).reshape(b, s_len, B_HEADS * B_HEAD_DIM)
    rows = min(WINDOW, s_len)
    return o @ w_out, k[:, s_len - rows:], v[:, s_len - rows:]


def swa_sample(h, cache_k, cache_v, w_in, b_in, sinks, w_out):
    b, t, _ = h.shape
    rows = cache_k.shape[1]
    qpos = PAST_LEN + jnp.arange(t)
    q, k, v = swa_project(h, w_in, b_in, qpos)
    kpos = jnp.concatenate([PAST_LEN - rows + jnp.arange(rows), qpos])
    qch, kch = qpos[:, None] // CHUNK, kpos[None, :] // CHUNK
    mask = ((kch >= qch - W_CHUNKS) & (kch <= qch))[None]
    k_all = jnp.concatenate([cache_k, k], axis=1)[:, None]
    v_all = jnp.concatenate([cache_v, v], axis=1)[:, None]
    qc = q.reshape(b, 1, t, B_KV_HEADS, B_GROUP, B_HEAD_DIM)
    o = sink_attend(qc, k_all, v_all, mask, sinks).reshape(b, t, B_HEADS * B_HEAD_DIM)
    return o @ w_out, k, v


def setup_inputs(seed: int = 0) -> dict:
    key = jax.random.key(seed)
    keys = jax.random.split(key, 32)
    counter = [0]

    def nrm(shape, scale=1.0):
        k = keys[counter[0]]
        counter[0] += 1
        return jax.random.normal(k, shape, jnp.float32) * scale

    swa_rows = min(WINDOW, PAST_LEN)
    d, f = D_MODEL, FFN_DIM
    return {
        'x_prompt': nrm((BATCH, SEQ, d)),
        'x_sample': nrm((DEC_BATCH, DEC_SEQ, d)),
        'cache_dsa_k': nrm((N_DSA_LAYERS, DEC_BATCH, PAST_LEN, A_KV_HEADS, A_HEAD_DIM)),
        'cache_dsa_v': nrm((N_DSA_LAYERS, DEC_BATCH, PAST_LEN, A_KV_HEADS, A_HEAD_DIM)),
        'cache_dsa_kidx': nrm((N_DSA_LAYERS, DEC_BATCH, PAST_LEN, IDX_DIM)),
        'cache_swa_k': nrm((N_SWA_LAYERS, DEC_BATCH, swa_rows, B_KV_HEADS, B_HEAD_DIM)),
        'cache_swa_v': nrm((N_SWA_LAYERS, DEC_BATCH, swa_rows, B_KV_HEADS, B_HEAD_DIM)),
        'norm_ffn1': 1.0 + nrm((DEPTH, d), 0.02),
        'ffn1_gate': nrm((DEPTH, d, f), d ** -0.5),
        'ffn1_up': nrm((DEPTH, d, f), d ** -0.5),
        'ffn1_down': nrm((DEPTH, f, d), f ** -0.5),
        'norm_mix': 1.0 + nrm((DEPTH, d), 0.02),
        'dsa_w_in': nrm((N_DSA_LAYERS, d, A_IN), d ** -0.5),
        'dsa_kln_g': 1.0 + nrm((N_DSA_LAYERS, IDX_DIM), 0.02),
        'dsa_kln_b': nrm((N_DSA_LAYERS, IDX_DIM), 0.02),
        'dsa_w_out': nrm((N_DSA_LAYERS, A_HEADS * A_HEAD_DIM, d), (A_HEADS * A_HEAD_DIM) ** -0.5),
        'swa_w_in': nrm((N_SWA_LAYERS, d, B_IN), d ** -0.5),
        'swa_b_in': nrm((N_SWA_LAYERS, B_IN), 0.02),
        'swa_sinks': nrm((N_SWA_LAYERS, B_HEADS), 0.5),
        'swa_w_out': nrm((N_SWA_LAYERS, B_HEADS * B_HEAD_DIM, d), (B_HEADS * B_HEAD_DIM) ** -0.5),
        'norm_ffn2': 1.0 + nrm((DEPTH, d), 0.02),
        'ffn2_gate': nrm((DEPTH, d, f), d ** -0.5),
        'ffn2_up': nrm((DEPTH, d, f), d ** -0.5),
        'ffn2_down': nrm((DEPTH, f, d), f ** -0.5),
        'norm_final': 1.0 + nrm((d,), 0.02),
    }


def reference(x_prompt, x_sample, cache_dsa_k, cache_dsa_v, cache_dsa_kidx, cache_swa_k, cache_swa_v,
              norm_ffn1, ffn1_gate, ffn1_up, ffn1_down, norm_mix,
              dsa_w_in, dsa_kln_g, dsa_kln_b, dsa_w_out,
              swa_w_in, swa_b_in, swa_sinks, swa_w_out,
              norm_ffn2, ffn2_gate, ffn2_up, ffn2_down, norm_final):
    xp, xs = x_prompt, x_sample
    dkp, dvp, dip, skp, svp = [], [], [], [], []
    dks, dvs, dis, sks, svs = [], [], [], [], []
    for i in range(DEPTH):
        xp = half_ffn(xp, norm_ffn1[i], ffn1_gate[i], ffn1_up[i], ffn1_down[i])
        xs = half_ffn(xs, norm_ffn1[i], ffn1_gate[i], ffn1_up[i], ffn1_down[i])
        hp = rms_norm(xp, norm_mix[i])
        hs = rms_norm(xs, norm_mix[i])
        j = i // N_MIXERS
        if i % N_MIXERS == 0:
            op, kp, vp, ikp = dsa_prompt(hp, dsa_w_in[j], dsa_kln_g[j], dsa_kln_b[j], dsa_w_out[j])
            os_, ks_, vs_, iks = dsa_sample(hs, cache_dsa_k[j], cache_dsa_v[j], cache_dsa_kidx[j],
                                           dsa_w_in[j], dsa_kln_g[j], dsa_kln_b[j], dsa_w_out[j])
            dkp.append(kp)
            dvp.append(vp)
            dip.append(ikp)
            dks.append(ks_)
            dvs.append(vs_)
            dis.append(iks)
        else:
            op, kp, vp = swa_prompt(hp, swa_w_in[j], swa_b_in[j], swa_sinks[j], swa_w_out[j])
            os_, ks_, vs_ = swa_sample(hs, cache_swa_k[j], cache_swa_v[j],
                                       swa_w_in[j], swa_b_in[j], swa_sinks[j], swa_w_out[j])
            skp.append(kp)
            svp.append(vp)
            sks.append(ks_)
            svs.append(vs_)
        xp = xp + op
        xs = xs + os_
        xp = half_ffn(xp, norm_ffn2[i], ffn2_gate[i], ffn2_up[i], ffn2_down[i])
        xs = half_ffn(xs, norm_ffn2[i], ffn2_gate[i], ffn2_up[i], ffn2_down[i])
    y_prompt = rms_norm(xp, norm_final)
    y_sample = rms_norm(xs, norm_final)
    new_dsa_k_prompt = jnp.stack(dkp)
    new_dsa_v_prompt = jnp.stack(dvp)
    new_dsa_kidx_prompt = jnp.stack(dip)
    new_swa_k_prompt = jnp.stack(skp)
    new_swa_v_prompt = jnp.stack(svp)
    new_dsa_k_sample = jnp.stack(dks)
    new_dsa_v_sample = jnp.stack(dvs)
    new_dsa_kidx_sample = jnp.stack(dis)
    new_swa_k_sample = jnp.stack(sks)
    new_swa_v_sample = jnp.stack(svs)
    return (y_prompt, y_sample, new_dsa_k_prompt, new_dsa_v_prompt, new_dsa_kidx_prompt,
            new_swa_k_prompt, new_swa_v_prompt, new_dsa_k_sample, new_dsa_v_sample,
            new_dsa_kidx_sample, new_swa_k_sample, new_swa_v_sample)
```

```python
import functools

import jax
import jax.numpy as jnp
import numpy as np
from jax import lax
from jax.experimental import pallas as pl
from jax.experimental.pallas import tpu as pltpu

F32 = jnp.float32
BF16 = jnp.bfloat16
I32 = jnp.int32

D_MODEL = 2048
SEQ = 8192
DEC_BATCH = 8
DEC_SEQ = 32
PAST_LEN = 1024
N_TOK = SEQ + DEC_BATCH * DEC_SEQ
CHUNK = 64
ROPE_THETA = 500000.0
NORM_EPS = 1e-6
FFN_DIM = 5632

A_HEADS = 16
A_KV_HEADS = 4
A_GROUP = A_HEADS // A_KV_HEADS
A_HEAD_DIM = 128
A_ROT = A_HEAD_DIM // 4
IDX_HEADS = 16
IDX_DIM = 128
TOPK_MAX = 256
A_SPLITS = (A_HEADS * A_HEAD_DIM, A_KV_HEADS * A_HEAD_DIM, A_KV_HEADS * A_HEAD_DIM,
            IDX_HEADS * IDX_DIM, IDX_DIM, IDX_HEADS)

B_HEADS = 32
B_KV_HEADS = 4
B_HEAD_DIM = 64
B_ROT = B_HEAD_DIM // 4
WINDOW = 128
B_SPLITS = (B_HEADS * B_HEAD_DIM, B_KV_HEADS * B_HEAD_DIM, B_KV_HEADS * B_HEAD_DIM)

LANES = 128
MXU_COLS = 256
VMEM_LIMIT = 56 * 1024 * 1024

FFN_TM = 704
FFN_TF = 512
PROJ_TM = 384
ATT_TQ = 128
ATT_TK = 512

INT_MIN = -2 ** 31
NEG_MASK = -2e30
NEG_INIT = -1e30


def _rms(x, g):
    return x * lax.rsqrt(jnp.mean(x * x, axis=-1, keepdims=True) + NORM_EPS) * g


def _ffn_kernel(x_ref, g_ref, wg_ref, wu_ref, wd_ref, gf_ref, o_ref, h_ref, *, final_norm):
    j = pl.program_id(1)

    @pl.when(j == 0)
    def _():
        h_ref[...] = _rms(x_ref[...], g_ref[...]).astype(BF16)
        o_ref[...] = jnp.zeros_like(o_ref)

    h = h_ref[...]
    g = jnp.dot(h, wg_ref[...], preferred_element_type=F32)
    u = jnp.dot(h, wu_ref[...], preferred_element_type=F32)
    a = (g * jax.nn.sigmoid(g) * u).astype(BF16)
    o_ref[...] += jnp.dot(a, wd_ref[...], preferred_element_type=F32)

    @pl.when(j == pl.num_programs(1) - 1)
    def _():
        y = x_ref[...] + 0.5 * o_ref[...]
        if final_norm:
            y = _rms(y, gf_ref[...])
        o_ref[...] = y


def _ffn(x, gain, wg, wu, wd, layer, final_gain=None):
    t = x.shape[0]
    final_norm = final_gain is not None
    gf = final_gain if final_norm else gain
    return pl.pallas_call(
        functools.partial(_ffn_kernel, final_norm=final_norm),
        grid=(t // FFN_TM, FFN_DIM // FFN_TF),
        in_specs=[
            pl.BlockSpec((FFN_TM, D_MODEL), lambda i, j: (i, 0)),
            pl.BlockSpec((1, D_MODEL), lambda i, j: (0, 0)),
            pl.BlockSpec((None, D_MODEL, FFN_TF), lambda i, j: (layer, 0, j)),
            pl.BlockSpec((None, D_MODEL, FFN_TF), lambda i, j: (layer, 0, j)),
            pl.BlockSpec((None, FFN_TF, D_MODEL), lambda i, j: (layer, j, 0)),
            pl.BlockSpec((1, D_MODEL), lambda i, j: (0, 0)),
        ],
        out_specs=pl.BlockSpec((FFN_TM, D_MODEL), lambda i, j: (i, 0)),
        out_shape=jax.ShapeDtypeStruct((t, D_MODEL), F32),
        scratch_shapes=[pltpu.VMEM((FFN_TM, D_MODEL), BF16)],
        compiler_params=pltpu.CompilerParams(
            dimension_semantics=("parallel", "arbitrary"), vmem_limit_bytes=VMEM_LIMIT),
        name="ffn",
    )(x, gain.reshape(1, D_MODEL), wg, wu, wd, gf.reshape(1, D_MODEL))


def _rope_tables(pos, rot, head_dim):
    half = rot // 2
    inv_freq = jnp.float32(ROPE_THETA) ** (-jnp.arange(half, dtype=F32) / half)
    ang = pos.astype(F32)[:, None] * inv_freq[None, :]
    cos, sin = jnp.cos(ang), jnp.sin(ang)
    t = pos.shape[0]
    pad = jnp.zeros((t, head_dim - rot), F32)
    c = jnp.concatenate([cos, cos, pad + 1.0], axis=1)
    sa = jnp.concatenate([-sin, jnp.zeros_like(sin), pad], axis=1)
    sb = jnp.concatenate([jnp.zeros_like(sin), sin, pad], axis=1)
    rep = LANES // head_dim
    return jnp.stack([jnp.tile(c, (1, rep)), jnp.tile(sa, (1, rep)), jnp.tile(sb, (1, rep))])


def _proj_kernel(*refs, segs, half, has_bias, has_ln):
    it = iter(refs)
    x_ref, g_ref, w_ref, tab_ref = next(it), next(it), next(it), next(it)
    b_ref = next(it) if has_bias else None
    lng_ref, lnb_ref = (next(it), next(it)) if has_ln else (None, None)
    out_refs = list(it)

    h = _rms(x_ref[...], g_ref[...]).astype(BF16)
    tm = h.shape[0]
    c_tab, sa_tab, sb_tab = tab_ref[0], tab_ref[1], tab_ref[2]

    for (col0, ncols, rope, ln, scale, head_major), o_ref in zip(segs, out_refs):
        for b0 in range(0, ncols, MXU_COLS):
            bw = min(MXU_COLS, ncols - b0)
            y = jnp.dot(h, w_ref[:, col0 + b0:col0 + b0 + bw], preferred_element_type=F32)
            if has_bias:
                y = y + b_ref[:, col0 + b0:col0 + b0 + bw]
            for s0 in range(0, bw, LANES):
                ys = y[:, s0:s0 + LANES]
                if ln:
                    mu = jnp.mean(ys, axis=-1, keepdims=True)
                    yc = ys - mu
                    var = jnp.mean(yc * yc, axis=-1, keepdims=True)
                    ys = yc * lax.rsqrt(var + NORM_EPS) * lng_ref[...] + lnb_ref[...]
                if rope:
                    ys = (ys * c_tab + pltpu.roll(ys, LANES - half, 1) * sa_tab
                          + pltpu.roll(ys, half, 1) * sb_tab)
                if scale is not None:
                    ys = ys * scale
                ys = ys.astype(o_ref.dtype)
                c = b0 + s0
                if head_major:
                    for t in range(tm // LANES):
                        o_ref[t, c // LANES] = ys[t * LANES:(t + 1) * LANES]
                else:
                    o_ref[:, c:c + LANES] = ys


def _proj(x, gain, w, tab, segs, outs, half, bias=None, ln=None):
    t = x.shape[0]
    nt = t // PROJ_TM
    wcols = w.shape[1]
    in_specs = [
        pl.BlockSpec((PROJ_TM, D_MODEL), lambda i: (i, 0)),
        pl.BlockSpec((1, D_MODEL), lambda i: (0, 0)),
        pl.BlockSpec((D_MODEL, wcols), lambda i: (0, 0), pipeline_mode=pl.Buffered(1)),
        pl.BlockSpec((3, PROJ_TM, LANES), lambda i: (0, i, 0)),
    ]
    args = [x, gain.reshape(1, D_MODEL), w, tab]
    if bias is not None:
        in_specs.append(pl.BlockSpec((1, wcols), lambda i: (0, 0)))
        args.append(bias.reshape(1, wcols))
    if ln is not None:
        in_specs += [pl.BlockSpec((1, LANES), lambda i: (0, 0))] * 2
        args += [ln[0].reshape(1, LANES), ln[1].reshape(1, LANES)]
    out_specs, out_shape = [], []
    tiles = PROJ_TM // LANES
    for ncols, dtype, head_major in outs:
        if head_major:
            nh = ncols // LANES
            out_specs.append(pl.BlockSpec((tiles, nh, LANES, LANES), lambda i: (i, 0, 0, 0)))
            out_shape.append(jax.ShapeDtypeStruct((t // LANES, nh, LANES, LANES), dtype))
        else:
            out_specs.append(pl.BlockSpec((PROJ_TM, ncols), lambda i: (i, 0)))
            out_shape.append(jax.ShapeDtypeStruct((t, ncols), dtype))
    return pl.pallas_call(
        functools.partial(_proj_kernel, segs=segs, half=half, has_bias=bias is not None,
                          has_ln=ln is not None),
        grid=(nt,),
        in_specs=in_specs,
        out_specs=out_specs,
        out_shape=out_shape,
        compiler_params=pltpu.CompilerParams(
            dimension_semantics=("parallel",), vmem_limit_bytes=VMEM_LIMIT),
        name="proj",
    )(*args)


def _outproj_kernel(x_ref, o_ref, w_ref, y_ref):
    y_ref[...] = x_ref[...] + jnp.dot(o_ref[...], w_ref[...], preferred_element_type=F32)


def _outproj(x, o, w):
    t = x.shape[0]
    return pl.pallas_call(
        _outproj_kernel,
        grid=(t // FFN_TM,),
        in_specs=[
            pl.BlockSpec((FFN_TM, D_MODEL), lambda i: (i, 0)),
            pl.BlockSpec((FFN_TM, o.shape[1]), lambda i: (i, 0)),
            pl.BlockSpec(w.shape, lambda i: (0, 0)),
        ],
        out_specs=pl.BlockSpec((FFN_TM, D_MODEL), lambda i: (i, 0)),
        out_shape=jax.ShapeDtypeStruct((t, D_MODEL), F32),
        compiler_params=pltpu.CompilerParams(
            dimension_semantics=("parallel",), vmem_limit_bytes=VMEM_LIMIT),
        name="outproj",
    )(x, o, w)


def _dsa_attn_kernel(qi_ref, q_ref, w_ref, kit_ref, kt_ref, v_ref, o_ref,
                     skey_ref, wexp_ref, m_ref, l_ref, acc_ref,
                     *, tq, tk, prompt, n_chunks_static, n_valid, topk):
    j = pl.program_id(0)
    sub = tk // LANES
    rows = lax.broadcasted_iota(I32, (tq, LANES), 0)
    lane = lax.broadcasted_iota(I32, (tq, LANES), 1)
    if prompt:
        limit = ((j * tq + rows) // CHUNK + 1) * CHUNK
        n_chunks = (j * tq + tq + tk - 1) // tk
    else:
        limit = jnp.full((tq, LANES), n_valid, I32)
        n_chunks = n_chunks_static

    w = w_ref[...]
    for h in range(IDX_HEADS):
        wexp_ref[h] = jnp.broadcast_to(w[:, h:h + 1], (tq, LANES))

    def score_chunk(c, carry):
        kit = kit_ref[c]
        accs = [jnp.zeros((tq, LANES), F32) for _ in range(sub)]
        for g in range(IDX_HEADS // A_GROUP):
            qs = qi_ref[A_GROUP * g:A_GROUP * (g + 1)].reshape(A_GROUP * tq, IDX_DIM)
            lg = jnp.dot(qs, kit, preferred_element_type=F32)
            for a in range(A_GROUP):
                wh = wexp_ref[A_GROUP * g + a]
                for s in range(sub):
                    accs[s] = accs[s] + wh * jnp.maximum(
                        lg[a * tq:(a + 1) * tq, s * LANES:(s + 1) * LANES], 0.0)
        for s in range(sub):
            bits = pltpu.bitcast(accs[s], I32)
            key = jnp.where(bits >= 0, bits, bits ^ jnp.int32(0x7FFFFFFF))
            col = c * tk + s * LANES + lane
            skey_ref[c, :, s * LANES:(s + 1) * LANES] = jnp.where(col < limit, key, jnp.int32(INT_MIN))
        return carry

    lax.fori_loop(0, n_chunks, score_chunk, 0)

    def bisect(it, prefix):
        bit = lax.shift_left(jnp.int32(1), (31 - it).astype(I32))
        cand_u = prefix | bit
        cand_b = jnp.broadcast_to(cand_u ^ jnp.int32(INT_MIN), (tq, LANES))

        def count_chunk(c, acc):
            blk = skey_ref[c]
            for s in range(sub):
                acc = acc + jnp.where(blk[:, s * LANES:(s + 1) * LANES] >= cand_b, 1.0, 0.0)
            return acc

        acc = lax.fori_loop(0, n_chunks, count_chunk, jnp.zeros((tq, LANES), F32))
        cnt = jnp.sum(acc, axis=1, keepdims=True)
        return jnp.where(cnt >= float(topk), cand_u, prefix)

    prefix = lax.fori_loop(0, 32, bisect, jnp.zeros((tq, 1), I32))
    tau = jnp.maximum(prefix ^ jnp.int32(INT_MIN), jnp.int32(INT_MIN + 1))
    tau_b = jnp.broadcast_to(tau, (tq, tk))

    m_ref[...] = jnp.full(m_ref.shape, NEG_INIT, F32)
    l_ref[...] = jnp.zeros(l_ref.shape, F32)
    acc_ref[...] = jnp.zeros(acc_ref.shape, F32)
    scale = A_HEAD_DIM ** -0.5

    def attn_chunk(c, carry):
        sel = skey_ref[c] >= tau_b
        for g in range(A_KV_HEADS):
            qs = q_ref[A_GROUP * g:A_GROUP * (g + 1)].reshape(A_GROUP * tq, A_HEAD_DIM)
            s_all = jnp.dot(qs, kt_ref[g, c], preferred_element_type=F32)
            ps, alphas = [], []
            for a in range(A_GROUP):
                h = A_GROUP * g + a
                sm = jnp.where(sel, s_all[a * tq:(a + 1) * tq] * scale, NEG_MASK)
                m_old = m_ref[h]
                m_new = jnp.maximum(m_old, jnp.max(sm, axis=1, keepdims=True))
                p = jnp.exp(sm - m_new)
                alpha = jnp.exp(m_old - m_new)
                l_ref[h] = alpha * l_ref[h] + jnp.sum(p, axis=1, keepdims=True)
                m_ref[h] = m_new
                ps.append(p.astype(BF16))
                alphas.append(alpha)
            pv = jnp.dot(jnp.concatenate(ps, axis=0), v_ref[g, c], preferred_element_type=F32)
            for a in range(A_GROUP):
                h = A_GROUP * g + a
                acc_ref[h] = alphas[a] * acc_ref[h] + pv[a * tq:(a + 1) * tq]
        return carry

    lax.fori_loop(0, n_chunks, attn_chunk, 0)

    for h in range(A_HEADS):
        o_ref[:, h * A_HEAD_DIM:(h + 1) * A_HEAD_DIM] = (acc_ref[h] / l_ref[h]).astype(o_ref.dtype)


def _dsa_attn(qi, q, w, kit, kt, v, *, tq, prompt, n_valid, topk):
    nt = q.shape[0]
    nb, nc = kit.shape[0], kit.shape[1]
    tk = ATT_TK
    if prompt:
        kmap3 = lambda j: (0, 0, 0, 0)
        kmap4 = lambda j: (0, 0, 0, 0, 0)
        mode = dict(pipeline_mode=pl.Buffered(1))
    else:
        kmap3 = lambda j: (j, 0, 0, 0)
        kmap4 = lambda j: (j, 0, 0, 0, 0)
        mode = {}
    return pl.pallas_call(
        functools.partial(_dsa_attn_kernel, tq=tq, tk=tk, prompt=prompt, n_chunks_static=nc,
                          n_valid=n_valid, topk=topk),
        grid=(nt,),
        in_specs=[
            pl.BlockSpec((None, IDX_HEADS, tq, IDX_DIM), lambda j: (j, 0, 0, 0)),
            pl.BlockSpec((None, A_HEADS, tq, A_HEAD_DIM), lambda j: (j, 0, 0, 0)),
            pl.BlockSpec((tq, LANES), lambda j: (j, 0)),
            pl.BlockSpec((None, nc, IDX_DIM, tk), kmap3, **mode),
            pl.BlockSpec((None, A_KV_HEADS, nc, A_HEAD_DIM, tk), kmap4, **mode),
            pl.BlockSpec((None, A_KV_HEADS, nc, tk, A_HEAD_DIM), kmap4, **mode),
        ],
        out_specs=pl.BlockSpec((tq, A_HEADS * A_HEAD_DIM), lambda j: (j, 0)),
        out_shape=jax.ShapeDtypeStruct((nt * tq, A_HEADS * A_HEAD_DIM), BF16),
        scratch_shapes=[
            pltpu.VMEM((nc, tq, tk), I32),
            pltpu.VMEM((IDX_HEADS, tq, LANES), F32),
            pltpu.VMEM((A_HEADS, tq, 1), F32),
            pltpu.VMEM((A_HEADS, tq, 1), F32),
            pltpu.VMEM((A_HEADS, tq, A_HEAD_DIM), F32),
        ],
        compiler_params=pltpu.CompilerParams(
            dimension_semantics=("parallel",), vmem_limit_bytes=VMEM_LIMIT),
        name="dsa_attn_prompt" if prompt else "dsa_attn_sample",
    )(qi, q, w, kit, kt, v)


def _dsa_key_layouts(ki, k, v, nc):
    b = ki.shape[0]
    tk = ATT_TK
    kit = ki.astype(BF16).reshape(b, nc, tk, IDX_DIM).transpose(0, 1, 3, 2)
    kt = k.astype(BF16).reshape(b, nc, tk, A_KV_HEADS, A_HEAD_DIM).transpose(0, 3, 1, 4, 2)
    vv = v.astype(BF16).reshape(b, nc, tk, A_KV_HEADS, A_HEAD_DIM).transpose(0, 3, 1, 2, 4)
    return kit, kt, vv


def _swa_attn_kernel(sink_ref, q_ref, ktp_ref, ktc_ref, vp_ref, vc_ref, o_ref, *, tq, prompt, n_valid):
    j = pl.program_id(0)
    nk = 2 * LANES
    kt = jnp.concatenate([ktp_ref[...], ktc_ref[...]], axis=1)
    vb = jnp.concatenate([vp_ref[...], vc_ref[...]], axis=0)
    row = lax.broadcasted_iota(I32, (tq, nk), 0)
    col = lax.broadcasted_iota(I32, (tq, nk), 1)
    if prompt:
        qc = row // CHUNK
        kc = col // CHUNK
        mask = (kc >= qc) & (kc <= qc + WINDOW // CHUNK) & ((j > 0) | (kc >= WINDOW // CHUNK))
    else:
        mask = col < n_valid
    vlane = lax.broadcasted_iota(I32, (nk, LANES), 1)
    zeros = jnp.zeros((B_HEAD_DIM, nk), BF16)
    scale = B_HEAD_DIM ** -0.5
    pairs_per_kv = (B_HEADS // B_KV_HEADS) // 2

    for g in range(B_KV_HEADS):
        ktg = kt[B_HEAD_DIM * g:B_HEAD_DIM * (g + 1)]
        kt_even = jnp.concatenate([ktg, zeros], axis=0)
        kt_odd = jnp.concatenate([zeros, ktg], axis=0)
        vblk = vb[:, (g // 2) * LANES:(g // 2 + 1) * LANES]
        vrol = pltpu.roll(vblk, B_HEAD_DIM, 1)
        lo, hi = (vblk, vrol) if g % 2 == 0 else (vrol, vblk)
        v_even = jnp.where(vlane < B_HEAD_DIM, lo, 0.0).astype(BF16)
        v_odd = jnp.where(vlane >= B_HEAD_DIM, hi, 0.0).astype(BF16)
        for hp in range(pairs_per_kv * g, pairs_per_kv * (g + 1)):
            qp = q_ref[:, hp * LANES:(hp + 1) * LANES]
            out = None
            for ktx, vx, h in ((kt_even, v_even, 2 * hp), (kt_odd, v_odd, 2 * hp + 1)):
                s = jnp.dot(qp, ktx, preferred_element_type=F32) * scale
                s = jnp.where(mask, s, NEG_MASK)
                sink = sink_ref[h]
                m = jnp.maximum(jnp.max(s, axis=1, keepdims=True), sink)
                p = jnp.exp(s - m)
                den = jnp.sum(p, axis=1, keepdims=True) + jnp.exp(sink - m)
                pv = jnp.dot((p / den).astype(BF16), vx, preferred_element_type=F32)
                out = pv if out is None else out + pv
            o_ref[:, hp * LANES:(hp + 1) * LANES] = out.astype(o_ref.dtype)


def _swa_attn(sinks, q, kt, v, *, tq, prompt, n_valid, q_block0):
    nt = (SEQ // tq) if prompt else DEC_BATCH
    if prompt:
        prev = lambda j: jnp.maximum(j - 1, 0)
        cur = lambda j: j
    else:
        prev = lambda j: 2 * j
        cur = lambda j: 2 * j + 1
    kdim = B_KV_HEADS * B_HEAD_DIM
    return pl.pallas_call(
        functools.partial(_swa_attn_kernel, tq=tq, prompt=prompt, n_valid=n_valid),
        grid=(nt,),
        in_specs=[
            pl.BlockSpec(memory_space=pltpu.SMEM),
            pl.BlockSpec((tq, B_HEADS * B_HEAD_DIM), lambda j: (q_block0 + j, 0)),
            pl.BlockSpec((kdim, LANES), lambda j: (0, prev(j))),
            pl.BlockSpec((kdim, LANES), lambda j: (0, cur(j))),
            pl.BlockSpec((LANES, kdim), lambda j: (prev(j), 0)),
            pl.BlockSpec((LANES, kdim), lambda j: (cur(j), 0)),
        ],
        out_specs=pl.BlockSpec((tq, B_HEADS * B_HEAD_DIM), lambda j: (j, 0)),
        out_shape=jax.ShapeDtypeStruct((nt * tq, B_HEADS * B_HEAD_DIM), BF16),
        compiler_params=pltpu.CompilerParams(
            dimension_semantics=("parallel",), vmem_limit_bytes=VMEM_LIMIT),
        name="swa_attn_prompt" if prompt else "swa_attn_sample",
    )(sinks, q, kt, kt, v, v)


def kernel(x_prompt, x_sample, cache_dsa_k, cache_dsa_v, cache_dsa_kidx, cache_swa_k, cache_swa_v,
           norm_ffn1, ffn1_gate, ffn1_up, ffn1_down, norm_mix,
           dsa_w_in, dsa_kln_g, dsa_kln_b, dsa_w_out,
           swa_w_in, swa_b_in, swa_sinks, swa_w_out,
           norm_ffn2, ffn2_gate, ffn2_up, ffn2_down, norm_final):
    n_s = DEC_BATCH * DEC_SEQ
    x = jnp.concatenate([x_prompt.reshape(SEQ, D_MODEL), x_sample.reshape(n_s, D_MODEL)], axis=0)
    pos = jnp.concatenate([jnp.arange(SEQ), PAST_LEN + jnp.tile(jnp.arange(DEC_SEQ), DEC_BATCH)])

    g1, u1, d1 = ffn1_gate.astype(BF16), ffn1_up.astype(BF16), ffn1_down.astype(BF16)
    g2, u2, d2 = ffn2_gate.astype(BF16), ffn2_up.astype(BF16), ffn2_down.astype(BF16)

    x = _ffn(x, norm_ffn1[0], g1, u1, d1, 0)

    a_in = sum(A_SPLITS)
    a_pad = -a_in % LANES
    w_in = jnp.pad(dsa_w_in[0], ((0, 0), (0, a_pad))).astype(BF16)
    offs = np.cumsum((0,) + A_SPLITS)
    w_scale = IDX_HEADS ** -0.5 * IDX_DIM ** -0.5
    segs = (
        (int(offs[0]), A_SPLITS[0], True, False, None, True),
        (int(offs[1]), A_SPLITS[1], True, False, None, False),
        (int(offs[2]), A_SPLITS[2], False, False, None, False),
        (int(offs[3]), A_SPLITS[3], True, False, None, True),
        (int(offs[4]), A_SPLITS[4], True, True, None, False),
        (int(offs[5]), LANES, False, False, w_scale, False),
    )
    outs = [(A_SPLITS[0], BF16, True), (A_SPLITS[1], F32, False), (A_SPLITS[2], F32, False),
            (A_SPLITS[3], BF16, True), (A_SPLITS[4], F32, False), (LANES, F32, False)]
    q_hm, k_a, v_a, qi_hm, ki_a, wh = _proj(
        x, norm_mix[0], w_in, _rope_tables(pos, A_ROT, A_HEAD_DIM), segs, outs, A_ROT // 2,
        ln=(dsa_kln_g[0], dsa_kln_b[0]))

    n_pt = SEQ // ATT_TQ
    kit, kt, vv = _dsa_key_layouts(ki_a[None, :SEQ], k_a[None, :SEQ], v_a[None, :SEQ], SEQ // ATT_TK)
    o_p = _dsa_attn(qi_hm[:n_pt], q_hm[:n_pt], wh[:SEQ], kit, kt, vv,
                    tq=ATT_TQ, prompt=True, n_valid=SEQ, topk=min(TOPK_MAX, SEQ // 4))

    l_s = PAST_LEN + DEC_SEQ
    nc_s = -(-l_s // ATT_TK)
    kpad = nc_s * ATT_TK - l_s

    def with_cache(cache, new):
        new = new[SEQ:].reshape(DEC_BATCH, DEC_SEQ, -1)
        cache = cache.reshape(DEC_BATCH, PAST_LEN, -1)
        return jnp.pad(jnp.concatenate([cache, new], axis=1), ((0, 0), (0, kpad), (0, 0)))

    kit_s, kt_s, vv_s = _dsa_key_layouts(with_cache(cache_dsa_kidx[0], ki_a), with_cache(cache_dsa_k[0], k_a),
                                         with_cache(cache_dsa_v[0], v_a), nc_s)

    def sample_heads(a):
        a = a[n_pt:].reshape(-1, A_HEADS, LANES // DEC_SEQ, DEC_SEQ, LANES)
        return a.transpose(0, 2, 1, 3, 4).reshape(DEC_BATCH, A_HEADS, DEC_SEQ, LANES)

    o_s = _dsa_attn(sample_heads(qi_hm), sample_heads(q_hm), wh[SEQ:], kit_s, kt_s, vv_s,
                    tq=DEC_SEQ, prompt=False, n_valid=l_s, topk=min(TOPK_MAX, l_s // 4))

    x = _outproj(x, jnp.concatenate([o_p, o_s], axis=0), dsa_w_out[0].astype(BF16))
    x = _ffn(x, norm_ffn2[0], g2, u2, d2, 0)

    x = _ffn(x, norm_ffn1[1], g1, u1, d1, 1)

    boffs = np.cumsum((0,) + B_SPLITS)
    bsegs = (
        (int(boffs[0]), B_SPLITS[0], True, False, None, False),
        (int(boffs[1]), B_SPLITS[1], True, False, None, False),
        (int(boffs[2]), B_SPLITS[2], False, False, None, False),
    )
    bouts = [(B_SPLITS[0], BF16, False), (B_SPLITS[1], F32, False), (B_SPLITS[2], F32, False)]
    q_b, k_b, v_b = _proj(x, norm_mix[1], swa_w_in[0].astype(BF16), _rope_tables(pos, B_ROT, B_HEAD_DIM),
                          bsegs, bouts, B_ROT // 2, bias=swa_b_in[0])

    sinks = swa_sinks[0]
    o_bp = _swa_attn(sinks, q_b, k_b[:SEQ].astype(BF16).T, v_b[:SEQ],
                     tq=ATT_TQ, prompt=True, n_valid=0, q_block0=0)

    rows = cache_swa_k.shape[2]
    spad = 2 * LANES - rows - DEC_SEQ

    def swa_keys(cache, new):
        new = new[SEQ:].reshape(DEC_BATCH, DEC_SEQ, -1)
        cache = cache.reshape(DEC_BATCH, rows, -1)
        full = jnp.pad(jnp.concatenate([cache, new], axis=1), ((0, 0), (0, spad), (0, 0)))
        return full.reshape(DEC_BATCH * 2 * LANES, -1)

    o_bs = _swa_attn(sinks, q_b, swa_keys(cache_swa_k[0], k_b).astype(BF16).T, swa_keys(cache_swa_v[0], v_b),
                     tq=DEC_SEQ, prompt=False, n_valid=rows + DEC_SEQ, q_block0=SEQ // DEC_SEQ)

    x = _outproj(x, jnp.concatenate([o_bp, o_bs], axis=0), swa_w_out[0].astype(BF16))
    y = _ffn(x, norm_ffn2[1], g2, u2, d2, 1, final_gain=norm_final)

    y_prompt = y[:SEQ].reshape(1, SEQ, D_MODEL)
    y_sample = y[SEQ:].reshape(DEC_BATCH, DEC_SEQ, D_MODEL)

    def split(a, *tail):
        return (a[:SEQ].reshape((1, 1, SEQ) + tail), a[SEQ:].reshape((1, DEC_BATCH, DEC_SEQ) + tail))

    dkp, dks = split(k_a, A_KV_HEADS, A_HEAD_DIM)
    dvp, dvs = split(v_a, A_KV_HEADS, A_HEAD_DIM)
    dip, dis = split(ki_a, IDX_DIM)
    skp, sks = split(k_b, B_KV_HEADS, B_HEAD_DIM)
    svp, svs = split(v_b, B_KV_HEADS, B_HEAD_DIM)
    wrows = min(WINDOW, SEQ)
    return (y_prompt, y_sample, dkp, dvp, dip, skp[:, :, SEQ - wrows:], svp[:, :, SEQ - wrows:],
            dks, dvs, dis, sks, svs)
```

```python
import functools
import math

import jax
import jax.numpy as jnp
import numpy as np
from jax import lax
from jax.experimental import pallas as pl
from jax.experimental.pallas import tpu as pltpu

F32 = jnp.float32
BF16 = jnp.bfloat16
I32 = jnp.int32

D_MODEL = 2048
SEQ = 8192
DEC_BATCH = 8
DEC_SEQ = 32
PAST_LEN = 1024
N_TOK = SEQ + DEC_BATCH * DEC_SEQ
CHUNK = 64
ROPE_THETA = 500000.0
NORM_EPS = 1e-6
FFN_DIM = 5632

A_HEADS = 16
A_KV_HEADS = 4
A_GROUP = A_HEADS // A_KV_HEADS
A_HEAD_DIM = 128
A_ROT = A_HEAD_DIM // 4
IDX_HEADS = 16
IDX_DIM = 128
TOPK_MAX = 256
A_SPLITS = (A_HEADS * A_HEAD_DIM, A_KV_HEADS * A_HEAD_DIM, A_KV_HEADS * A_HEAD_DIM,
            IDX_HEADS * IDX_DIM, IDX_DIM, IDX_HEADS)

B_HEADS = 32
B_KV_HEADS = 4
B_HEAD_DIM = 64
B_ROT = B_HEAD_DIM // 4
WINDOW = 128
B_SPLITS = (B_HEADS * B_HEAD_DIM, B_KV_HEADS * B_HEAD_DIM, B_KV_HEADS * B_HEAD_DIM)

LANES = 128
MXU_COLS = 256
VMEM_LIMIT = 56 * 1024 * 1024

FFN_TM = 704
FFN_TF = 512
PROJ_TM = 384
ATT_TQ = 256
ATT_TK = 512
SWA_TQ = 128
RED_ROWS = 32

INT_MIN = -2 ** 31
NEG_MASK = -2e30
NEG_INIT = -1e30
LOG2_E = math.log2(math.e)


def _rms(x, g):
    return x * lax.rsqrt(jnp.mean(x * x, axis=-1, keepdims=True) + NORM_EPS) * g


def _ffn_kernel(x_ref, g_ref, wg_ref, wu_ref, wd_ref, gf_ref, o_ref, h_ref, *, final_norm):
    j = pl.program_id(1)

    @pl.when(j == 0)
    def _():
        h_ref[...] = _rms(x_ref[...], g_ref[...]).astype(BF16)
        o_ref[...] = jnp.zeros_like(o_ref)

    h = h_ref[...]
    g = jnp.dot(h, wg_ref[...], preferred_element_type=F32)
    u = jnp.dot(h, wu_ref[...], preferred_element_type=F32)
    a = (g * jax.nn.sigmoid(g) * u).astype(BF16)
    o_ref[...] += jnp.dot(a, wd_ref[...], preferred_element_type=F32)

    @pl.when(j == pl.num_programs(1) - 1)
    def _():
        y = x_ref[...] + 0.5 * o_ref[...]
        if final_norm:
            y = _rms(y, gf_ref[...])
        o_ref[...] = y


def _ffn(x, gain, wg, wu, wd, layer, final_gain=None):
    t = x.shape[0]
    final_norm = final_gain is not None
    gf = final_gain if final_norm else gain
    return pl.pallas_call(
        functools.partial(_ffn_kernel, final_norm=final_norm),
        grid=(t // FFN_TM, FFN_DIM // FFN_TF),
        in_specs=[
            pl.BlockSpec((FFN_TM, D_MODEL), lambda i, j: (i, 0)),
            pl.BlockSpec((1, D_MODEL), lambda i, j: (0, 0)),
            pl.BlockSpec((None, D_MODEL, FFN_TF), lambda i, j: (layer, 0, j)),
            pl.BlockSpec((None, D_MODEL, FFN_TF), lambda i, j: (layer, 0, j)),
            pl.BlockSpec((None, FFN_TF, D_MODEL), lambda i, j: (layer, j, 0)),
            pl.BlockSpec((1, D_MODEL), lambda i, j: (0, 0)),
        ],
        out_specs=pl.BlockSpec((FFN_TM, D_MODEL), lambda i, j: (i, 0)),
        out_shape=jax.ShapeDtypeStruct((t, D_MODEL), F32),
        scratch_shapes=[pltpu.VMEM((FFN_TM, D_MODEL), BF16)],
        compiler_params=pltpu.CompilerParams(
            dimension_semantics=("parallel", "arbitrary"), vmem_limit_bytes=VMEM_LIMIT),
        name="ffn",
    )(x, gain.reshape(1, D_MODEL), wg, wu, wd, gf.reshape(1, D_MODEL))


def _rope_tables(pos, rot, head_dim):
    half = rot // 2
    inv_freq = jnp.float32(ROPE_THETA) ** (-jnp.arange(half, dtype=F32) / half)
    ang = pos.astype(F32)[:, None] * inv_freq[None, :]
    cos, sin = jnp.cos(ang), jnp.sin(ang)
    t = pos.shape[0]
    pad = jnp.zeros((t, head_dim - rot), F32)
    c = jnp.concatenate([cos, cos, pad + 1.0], axis=1)
    sa = jnp.concatenate([-sin, jnp.zeros_like(sin), pad], axis=1)
    sb = jnp.concatenate([jnp.zeros_like(sin), sin, pad], axis=1)
    rep = LANES // head_dim
    return jnp.stack([jnp.tile(c, (1, rep)), jnp.tile(sa, (1, rep)), jnp.tile(sb, (1, rep))])


def _proj_kernel(*refs, segs, half, has_bias, has_ln):
    it = iter(refs)
    x_ref, g_ref, w_ref, tab_ref = next(it), next(it), next(it), next(it)
    b_ref = next(it) if has_bias else None
    lng_ref, lnb_ref = (next(it), next(it)) if has_ln else (None, None)
    out_refs = list(it)

    h = _rms(x_ref[...], g_ref[...]).astype(BF16)
    c_tab, sa_tab, sb_tab = tab_ref[0], tab_ref[1], tab_ref[2]

    for (col0, ncols, rope, ln, scale), o_ref in zip(segs, out_refs):
        for b0 in range(0, ncols, MXU_COLS):
            bw = min(MXU_COLS, ncols - b0)
            y = jnp.dot(h, w_ref[:, col0 + b0:col0 + b0 + bw], preferred_element_type=F32)
            if has_bias:
                y = y + b_ref[:, col0 + b0:col0 + b0 + bw]
            for s0 in range(0, bw, LANES):
                ys = y[:, s0:s0 + LANES]
                if ln:
                    mu = jnp.mean(ys, axis=-1, keepdims=True)
                    yc = ys - mu
                    var = jnp.mean(yc * yc, axis=-1, keepdims=True)
                    ys = yc * lax.rsqrt(var + NORM_EPS) * lng_ref[...] + lnb_ref[...]
                if rope:
                    ys = (ys * c_tab + pltpu.roll(ys, LANES - half, 1) * sa_tab
                          + pltpu.roll(ys, half, 1) * sb_tab)
                if scale is not None:
                    ys = ys * scale
                o_ref[:, b0 + s0:b0 + s0 + LANES] = ys.astype(o_ref.dtype)


def _proj(x, gain, w, tab, segs, outs, half, bias=None, ln=None):
    t = x.shape[0]
    wcols = w.shape[1]
    in_specs = [
        pl.BlockSpec((PROJ_TM, D_MODEL), lambda i: (i, 0)),
        pl.BlockSpec((1, D_MODEL), lambda i: (0, 0)),
        pl.BlockSpec((D_MODEL, wcols), lambda i: (0, 0), pipeline_mode=pl.Buffered(1)),
        pl.BlockSpec((3, PROJ_TM, LANES), lambda i: (0, i, 0)),
    ]
    args = [x, gain.reshape(1, D_MODEL), w, tab]
    if bias is not None:
        in_specs.append(pl.BlockSpec((1, wcols), lambda i: (0, 0)))
        args.append(bias.reshape(1, wcols))
    if ln is not None:
        in_specs += [pl.BlockSpec((1, LANES), lambda i: (0, 0))] * 2
        args += [ln[0].reshape(1, LANES), ln[1].reshape(1, LANES)]
    out_specs = [pl.BlockSpec((PROJ_TM, ncols), lambda i: (i, 0)) for ncols, _ in outs]
    out_shape = [jax.ShapeDtypeStruct((t, ncols), dtype) for ncols, dtype in outs]
    return pl.pallas_call(
        functools.partial(_proj_kernel, segs=segs, half=half, has_bias=bias is not None,
                          has_ln=ln is not None),
        grid=(t // PROJ_TM,),
        in_specs=in_specs,
        out_specs=out_specs,
        out_shape=out_shape,
        compiler_params=pltpu.CompilerParams(
            dimension_semantics=("parallel",), vmem_limit_bytes=VMEM_LIMIT),
        name="proj",
    )(*args)


def _outproj_kernel(x_ref, o_ref, w_ref, y_ref):
    y_ref[...] = x_ref[...] + jnp.dot(o_ref[...], w_ref[...], preferred_element_type=F32)


def _outproj(x, o, w):
    t = x.shape[0]
    return pl.pallas_call(
        _outproj_kernel,
        grid=(t // FFN_TM,),
        in_specs=[
            pl.BlockSpec((FFN_TM, D_MODEL), lambda i: (i, 0)),
            pl.BlockSpec((FFN_TM, o.shape[1]), lambda i: (i, 0)),
            pl.BlockSpec(w.shape, lambda i: (0, 0)),
        ],
        out_specs=pl.BlockSpec((FFN_TM, D_MODEL), lambda i: (i, 0)),
        out_shape=jax.ShapeDtypeStruct((t, D_MODEL), F32),
        compiler_params=pltpu.CompilerParams(
            dimension_semantics=("parallel",), vmem_limit_bytes=VMEM_LIMIT),
        name="outproj",
    )(x, o, w)


def _key_to_float(key):
    bits = jnp.where(key >= 0, key, key ^ jnp.int32(0x7FFFFFFF))
    exp_mask = jnp.int32(0x7F800000)
    bits = jnp.where((bits & exp_mask) == exp_mask, bits & jnp.int32(-0x800000), bits)
    return pltpu.bitcast(bits, F32)


def _col_reduce(x, op):
    r, n = x.shape
    return op(op(x.reshape(r // RED_ROWS, RED_ROWS, n), axis=0), axis=0, keepdims=True)


def _dsa_attn_kernel(qit_ref, qt_ref, wt_ref, ki_ref, k_ref, vt_ref, ot_ref,
                     s_ref, bias_ref, m_ref, l_ref, acc_ref,
                     *, tq, tk, prompt, n_chunks_static, n_valid, topk):
    j = pl.program_id(0)
    sub = tk // LANES
    qlane = lax.broadcasted_iota(I32, (1, tq), 1)
    if prompt:
        limit = ((j * tq + qlane) // CHUNK + 1) * CHUNK
        n_chunks = (j * tq + tq + tk - 1) // tk
    else:
        limit = jnp.full((1, tq), n_valid, I32)
        n_chunks = n_chunks_static
    krow = lax.broadcasted_iota(I32, (LANES, tq), 0)

    def score_chunk(c, carry):
        for s in range(sub):
            kis = ki_ref[c, s * LANES:(s + 1) * LANES, :]
            acc = jnp.zeros((LANES, tq), F32)
            for h in range(IDX_HEADS):
                lg = jnp.dot(kis, qit_ref[h], preferred_element_type=F32)
                acc = acc + wt_ref[h:h + 1, :] * jnp.maximum(lg, 0.0)
            kidx = c * tk + s * LANES + krow
            s_ref[c, s * LANES:(s + 1) * LANES, :] = jnp.where(kidx < limit, acc, -jnp.inf)
        return carry

    lax.fori_loop(0, n_chunks, score_chunk, 0)

    def bisect(it, prefix):
        bit = lax.shift_left(jnp.int32(1), (31 - it).astype(I32))
        cand_u = prefix | bit
        cand_f = _key_to_float(cand_u ^ jnp.int32(INT_MIN))

        def count_chunk(c, acc):
            ones = jnp.where(s_ref[c] >= cand_f, 1.0, 0.0)
            return acc + jnp.sum(ones.reshape(tk // RED_ROWS, RED_ROWS, tq), axis=0)

        acc = lax.fori_loop(0, n_chunks, count_chunk, jnp.zeros((RED_ROWS, tq), F32))
        cnt = jnp.sum(acc, axis=0, keepdims=True)
        return jnp.where(cnt >= float(topk), cand_u, prefix)

    prefix = lax.fori_loop(0, 32, bisect, jnp.zeros((1, tq), I32))
    tau_f = _key_to_float(jnp.maximum(prefix ^ jnp.int32(INT_MIN), jnp.int32(INT_MIN + 0x00800000)))

    m_ref[...] = jnp.full(m_ref.shape, NEG_INIT, F32)
    l_ref[...] = jnp.zeros(l_ref.shape, F32)
    acc_ref[...] = jnp.zeros(acc_ref.shape, F32)
    c_exp = A_HEAD_DIM ** -0.5 * LOG2_E

    def attn_chunk(c, carry):
        bias_ref[...] = jnp.where(s_ref[c] >= tau_f, 0.0, NEG_MASK)
        m_old = [m_ref[h] for h in range(A_HEADS)]
        l_old = [l_ref[h] for h in range(A_HEADS)]
        m_new, l_new = [], []

        def logits(h):
            kc = k_ref[c, :, (h // A_GROUP) * A_HEAD_DIM:(h // A_GROUP + 1) * A_HEAD_DIM]
            return jnp.dot(kc, qt_ref[h], preferred_element_type=F32) + bias_ref[...]

        sb = logits(0)
        for h in range(A_HEADS):
            sb_next = logits(h + 1) if h + 1 < A_HEADS else None
            mn = jnp.maximum(m_old[h], _col_reduce(sb, jnp.max))
            p = jnp.exp2((sb - mn) * c_exp)
            alpha = jnp.exp2((m_old[h] - mn) * c_exp)
            m_new.append(mn)
            l_new.append(alpha * l_old[h] + _col_reduce(p, jnp.sum))
            pv = jnp.dot(vt_ref[h // A_GROUP, c], p.astype(BF16), preferred_element_type=F32)
            acc_ref[h] = alpha * acc_ref[h] + pv
            sb = sb_next
        for h in range(A_HEADS):
            m_ref[h] = m_new[h]
            l_ref[h] = l_new[h]
        return carry

    lax.fori_loop(0, n_chunks, attn_chunk, 0)

    for h in range(A_HEADS):
        ot_ref[h] = (acc_ref[h] / l_ref[h]).astype(ot_ref.dtype)


def _dsa_attn(qit, qt, wt, ki, k, vt, *, tq, prompt, n_valid, topk):
    nb, nc = ki.shape[0], ki.shape[1]
    tk = ATT_TK
    if prompt:
        nt = qt.shape[3] // tq
        qmap = lambda j: (0, 0, 0, j)
        wmap = lambda j: (0, 0, j)
        kmap4 = lambda j: (0, 0, 0, 0)
        kmap5 = lambda j: (0, 0, 0, 0, 0)
        mode = dict(pipeline_mode=pl.Buffered(1))
    else:
        nt = nb
        qmap = lambda j: (j, 0, 0, 0)
        wmap = lambda j: (j, 0, 0)
        kmap4 = lambda j: (j, 0, 0, 0)
        kmap5 = lambda j: (j, 0, 0, 0, 0)
        mode = {}
    return pl.pallas_call(
        functools.partial(_dsa_attn_kernel, tq=tq, tk=tk, prompt=prompt, n_chunks_static=nc,
                          n_valid=n_valid, topk=topk),
        grid=(nt,),
        in_specs=[
            pl.BlockSpec((None, IDX_HEADS, IDX_DIM, tq), qmap),
            pl.BlockSpec((None, A_HEADS, A_HEAD_DIM, tq), qmap),
            pl.BlockSpec((None, IDX_HEADS, tq), wmap),
            pl.BlockSpec((None, nc, tk, IDX_DIM), kmap4, **mode),
            pl.BlockSpec((None, nc, tk, A_KV_HEADS * A_HEAD_DIM), kmap4, **mode),
            pl.BlockSpec((None, A_KV_HEADS, nc, A_HEAD_DIM, tk), kmap5, **mode),
        ],
        out_specs=pl.BlockSpec((None, A_HEADS, A_HEAD_DIM, tq), qmap),
        out_shape=jax.ShapeDtypeStruct(qt.shape, BF16),
        scratch_shapes=[
            pltpu.VMEM((nc, tk, tq), F32),
            pltpu.VMEM((tk, tq), F32),
            pltpu.VMEM((A_HEADS, 1, tq), F32),
            pltpu.VMEM((A_HEADS, 1, tq), F32),
            pltpu.VMEM((A_HEADS, A_HEAD_DIM, tq), F32),
        ],
        compiler_params=pltpu.CompilerParams(
            dimension_semantics=("parallel",), vmem_limit_bytes=VMEM_LIMIT),
        name="dsa_attn_prompt" if prompt else "dsa_attn_sample",
    )(qit, qt, wt, ki, k, vt)


def _dsa_key_layouts(ki, k, v, nc):
    b = ki.shape[0]
    tk = ATT_TK
    kic = ki.astype(BF16).reshape(b, nc, tk, IDX_DIM)
    kc = k.astype(BF16).reshape(b, nc, tk, A_KV_HEADS * A_HEAD_DIM)
    vt = v.astype(BF16).reshape(b, nc, tk, A_KV_HEADS, A_HEAD_DIM).transpose(0, 3, 1, 4, 2)
    return kic, kc, vt


def _heads_t(a, lanes):
    b, t, _ = a.shape
    a = a.reshape(b, t, -1, LANES).transpose(0, 2, 3, 1)
    return jnp.pad(a, ((0, 0), (0, 0), (0, 0), (0, lanes - t)))


def _heads_untranspose(ot, t):
    b, h, d, _ = ot.shape
    return ot[..., :t].transpose(0, 3, 1, 2).reshape(b * t, h * d)


def _swa_attn_kernel(sink_ref, q_ref, ktp_ref, ktc_ref, vp_ref, vc_ref, o_ref, *, tq, prompt, n_valid):
    j = pl.program_id(0)
    nk = 2 * LANES
    kt = jnp.concatenate([ktp_ref[...], ktc_ref[...]], axis=1)
    vb = jnp.concatenate([vp_ref[...], vc_ref[...]], axis=0)
    row = lax.broadcasted_iota(I32, (tq, nk), 0)
    col = lax.broadcasted_iota(I32, (tq, nk), 1)
    if prompt:
        qc = row // CHUNK
        kc = col // CHUNK
        mask = (kc >= qc) & (kc <= qc + WINDOW // CHUNK) & ((j > 0) | (kc >= WINDOW // CHUNK))
    else:
        mask = col < n_valid
    vlane = lax.broadcasted_iota(I32, (nk, LANES), 1)
    zeros = jnp.zeros((B_HEAD_DIM, nk), BF16)
    scale = B_HEAD_DIM ** -0.5

    kts, vs = [], []
    for g in range(B_KV_HEADS):
        ktg = kt[B_HEAD_DIM * g:B_HEAD_DIM * (g + 1)]
        kts.append((jnp.concatenate([ktg, zeros], axis=0), jnp.concatenate([zeros, ktg], axis=0)))
        vblk = vb[:, (g // 2) * LANES:(g // 2 + 1) * LANES]
        vrol = pltpu.roll(vblk, B_HEAD_DIM, 1)
        lo, hi = (vblk, vrol) if g % 2 == 0 else (vrol, vblk)
        vs.append((jnp.where(vlane < B_HEAD_DIM, lo, 0.0).astype(BF16),
                   jnp.where(vlane >= B_HEAD_DIM, hi, 0.0).astype(BF16)))

    def logits(h):
        qp = q_ref[:, (h // 2) * LANES:(h // 2 + 1) * LANES]
        return jnp.dot(qp, kts[h // (B_HEADS // B_KV_HEADS)][h % 2], preferred_element_type=F32)

    s = logits(0)
    out = None
    for h in range(B_HEADS):
        s_next = logits(h + 1) if h + 1 < B_HEADS else None
        s = jnp.where(mask, s * scale, NEG_MASK)
        sink = sink_ref[h]
        m = jnp.maximum(jnp.max(s, axis=1, keepdims=True), sink)
        p = jnp.exp(s - m)
        den = jnp.sum(p, axis=1, keepdims=True) + jnp.exp(sink - m)
        pv = jnp.dot((p / den).astype(BF16), vs[h // (B_HEADS // B_KV_HEADS)][h % 2],
                     preferred_element_type=F32)
        if h % 2 == 0:
            out = pv
        else:
            o_ref[:, (h // 2) * LANES:(h // 2 + 1) * LANES] = (out + pv).astype(o_ref.dtype)
        s = s_next


def _swa_attn(sinks, q, kt, v, *, tq, prompt, n_valid, q_block0):
    nt = (SEQ // tq) if prompt else DEC_BATCH
    if prompt:
        prev = lambda j: jnp.maximum(j - 1, 0)
        cur = lambda j: j
    else:
        prev = lambda j: 2 * j
        cur = lambda j: 2 * j + 1
    kdim = B_KV_HEADS * B_HEAD_DIM
    return pl.pallas_call(
        functools.partial(_swa_attn_kernel, tq=tq, prompt=prompt, n_valid=n_valid),
        grid=(nt,),
        in_specs=[
            pl.BlockSpec(memory_space=pltpu.SMEM),
            pl.BlockSpec((tq, B_HEADS * B_HEAD_DIM), lambda j: (q_block0 + j, 0)),
            pl.BlockSpec((kdim, LANES), lambda j: (0, prev(j))),
            pl.BlockSpec((kdim, LANES), lambda j: (0, cur(j))),
            pl.BlockSpec((LANES, kdim), lambda j: (prev(j), 0)),
            pl.BlockSpec((LANES, kdim), lambda j: (cur(j), 0)),
        ],
        out_specs=pl.BlockSpec((tq, B_HEADS * B_HEAD_DIM), lambda j: (j, 0)),
        out_shape=jax.ShapeDtypeStruct((nt * tq, B_HEADS * B_HEAD_DIM), BF16),
        compiler_params=pltpu.CompilerParams(
            dimension_semantics=("parallel",), vmem_limit_bytes=VMEM_LIMIT),
        name="swa_attn_prompt" if prompt else "swa_attn_sample",
    )(sinks, q, kt, kt, v, v)


def kernel(x_prompt, x_sample, cache_dsa_k, cache_dsa_v, cache_dsa_kidx, cache_swa_k, cache_swa_v,
           norm_ffn1, ffn1_gate, ffn1_up, ffn1_down, norm_mix,
           dsa_w_in, dsa_kln_g, dsa_kln_b, dsa_w_out,
           swa_w_in, swa_b_in, swa_sinks, swa_w_out,
           norm_ffn2, ffn2_gate, ffn2_up, ffn2_down, norm_final):
    n_s = DEC_BATCH * DEC_SEQ
    x = jnp.concatenate([x_prompt.reshape(SEQ, D_MODEL), x_sample.reshape(n_s, D_MODEL)], axis=0)
    pos = jnp.concatenate([jnp.arange(SEQ), PAST_LEN + jnp.tile(jnp.arange(DEC_SEQ), DEC_BATCH)])

    g1, u1, d1 = ffn1_gate.astype(BF16), ffn1_up.astype(BF16), ffn1_down.astype(BF16)
    g2, u2, d2 = ffn2_gate.astype(BF16), ffn2_up.astype(BF16), ffn2_down.astype(BF16)

    x = _ffn(x, norm_ffn1[0], g1, u1, d1, 0)

    a_in = sum(A_SPLITS)
    a_pad = -a_in % LANES
    w_in = jnp.pad(dsa_w_in[0], ((0, 0), (0, a_pad))).astype(BF16)
    offs = np.cumsum((0,) + A_SPLITS)
    w_scale = IDX_HEADS ** -0.5 * IDX_DIM ** -0.5
    segs = (
        (int(offs[0]), A_SPLITS[0], True, False, None),
        (int(offs[1]), A_SPLITS[1], True, False, None),
        (int(offs[2]), A_SPLITS[2], False, False, None),
        (int(offs[3]), A_SPLITS[3], True, False, None),
        (int(offs[4]), A_SPLITS[4], True, True, None),
        (int(offs[5]), LANES, False, False, w_scale),
    )
    outs = [(A_SPLITS[0], BF16), (A_SPLITS[1], F32), (A_SPLITS[2], F32),
            (A_SPLITS[3], BF16), (A_SPLITS[4], F32), (LANES, F32)]
    q_a, k_a, v_a, qi_a, ki_a, wh = _proj(
        x, norm_mix[0], w_in, _rope_tables(pos, A_ROT, A_HEAD_DIM), segs, outs, A_ROT // 2,
        ln=(dsa_kln_g[0], dsa_kln_b[0]))
    wh = wh[:, :IDX_HEADS]

    kic, kc, vt = _dsa_key_layouts(ki_a[None, :SEQ], k_a[None, :SEQ], v_a[None, :SEQ], SEQ // ATT_TK)
    ot_p = _dsa_attn(_heads_t(qi_a[None, :SEQ], SEQ), _heads_t(q_a[None, :SEQ], SEQ), wh[:SEQ].T[None],
                     kic, kc, vt, tq=ATT_TQ, prompt=True, n_valid=SEQ, topk=min(TOPK_MAX, SEQ // 4))

    l_s = PAST_LEN + DEC_SEQ
    nc_s = -(-l_s // ATT_TK)
    kpad = nc_s * ATT_TK - l_s

    def with_cache(cache, new):
        new = new[SEQ:].reshape(DEC_BATCH, DEC_SEQ, -1)
        cache = cache.reshape(DEC_BATCH, PAST_LEN, -1)
        return jnp.pad(jnp.concatenate([cache, new], axis=1), ((0, 0), (0, kpad), (0, 0)))

    kic_s, kc_s, vt_s = _dsa_key_layouts(with_cache(cache_dsa_kidx[0], ki_a), with_cache(cache_dsa_k[0], k_a),
                                         with_cache(cache_dsa_v[0], v_a), nc_s)
    wt_s = jnp.pad(wh[SEQ:].reshape(DEC_BATCH, DEC_SEQ, IDX_HEADS).transpose(0, 2, 1),
                   ((0, 0), (0, 0), (0, LANES - DEC_SEQ)))
    ot_s = _dsa_attn(_heads_t(qi_a[SEQ:].reshape(DEC_BATCH, DEC_SEQ, -1), LANES),
                     _heads_t(q_a[SEQ:].reshape(DEC_BATCH, DEC_SEQ, -1), LANES), wt_s,
                     kic_s, kc_s, vt_s, tq=LANES, prompt=False, n_valid=l_s, topk=min(TOPK_MAX, l_s // 4))

    o_a = jnp.concatenate([_heads_untranspose(ot_p, SEQ), _heads_untranspose(ot_s, DEC_SEQ)], axis=0)
    x = _outproj(x, o_a, dsa_w_out[0].astype(BF16))
    x = _ffn(x, norm_ffn2[0], g2, u2, d2, 0)

    x = _ffn(x, norm_ffn1[1], g1, u1, d1, 1)

    boffs = np.cumsum((0,) + B_SPLITS)
    bsegs = (
        (int(boffs[0]), B_SPLITS[0], True, False, None),
        (int(boffs[1]), B_SPLITS[1], True, False, None),
        (int(boffs[2]), B_SPLITS[2], False, False, None),
    )
    bouts = [(B_SPLITS[0], BF16), (B_SPLITS[1], F32), (B_SPLITS[2], F32)]
    q_b, k_b, v_b = _proj(x, norm_mix[1], swa_w_in[0].astype(BF16), _rope_tables(pos, B_ROT, B_HEAD_DIM),
                          bsegs, bouts, B_ROT // 2, bias=swa_b_in[0])

    sinks = swa_sinks[0]
    o_bp = _swa_attn(sinks, q_b, k_b[:SEQ].astype(BF16).T, v_b[:SEQ],
                     tq=SWA_TQ, prompt=True, n_valid=0, q_block0=0)

    rows = cache_swa_k.shape[2]
    spad = 2 * LANES - rows - DEC_SEQ

    def swa_keys(cache, new):
        new = new[SEQ:].reshape(DEC_BATCH, DEC_SEQ, -1)
        cache = cache.reshape(DEC_BATCH, rows, -1)
        full = jnp.pad(jnp.concatenate([cache, new], axis=1), ((0, 0), (0, spad), (0, 0)))
        return full.reshape(DEC_BATCH * 2 * LANES, -1)

    o_bs = _swa_attn(sinks, q_b, swa_keys(cache_swa_k[0], k_b).astype(BF16).T, swa_keys(cache_swa_v[0], v_b),
                     tq=DEC_SEQ, prompt=False, n_valid=rows + DEC_SEQ, q_block0=SEQ // DEC_SEQ)

    x = _outproj(x, jnp.concatenate([o_bp, o_bs], axis=0), swa_w_out[0].astype(BF16))
    y = _ffn(x, norm_ffn2[1], g2, u2, d2, 1, final_gain=norm_final)

    y_prompt = y[:SEQ].reshape(1, SEQ, D_MODEL)
    y_sample = y[SEQ:].reshape(DEC_BATCH, DEC_SEQ, D_MODEL)

    def split(a, *tail):
        return (a[:SEQ].reshape((1, 1, SEQ) + tail), a[SEQ:].reshape((1, DEC_BATCH, DEC_SEQ) + tail))

    dkp, dks = split(k_a, A_KV_HEADS, A_HEAD_DIM)
    dvp, dvs = split(v_a, A_KV_HEADS, A_HEAD_DIM)
    dip, dis = split(ki_a, IDX_DIM)
    skp, sks = split(k_b, B_KV_HEADS, B_HEAD_DIM)
    svp, svs = split(v_b, B_KV_HEADS, B_HEAD_DIM)
    wrows = min(WINDOW, SEQ)
    return (y_prompt, y_sample, dkp, dvp, dip, skp[:, :, SEQ - wrows:], svp[:, :, SEQ - wrows:],
            dks, dvs, dis, sks, svs)
```

```python
import functools
import math

import jax
import jax.numpy as jnp
import numpy as np
from jax import lax
from jax.experimental import pallas as pl
from jax.experimental.pallas import tpu as pltpu

F32 = jnp.float32
BF16 = jnp.bfloat16
I32 = jnp.int32

D_MODEL = 2048
SEQ = 8192
DEC_BATCH = 8
DEC_SEQ = 32
PAST_LEN = 1024
N_TOK = SEQ + DEC_BATCH * DEC_SEQ
CHUNK = 64
ROPE_THETA = 500000.0
NORM_EPS = 1e-6
FFN_DIM = 5632

A_HEADS = 16
A_KV_HEADS = 4
A_GROUP = A_HEADS // A_KV_HEADS
A_HEAD_DIM = 128
A_ROT = A_HEAD_DIM // 4
IDX_HEADS = 16
IDX_DIM = 128
TOPK_MAX = 256
A_SPLITS = (A_HEADS * A_HEAD_DIM, A_KV_HEADS * A_HEAD_DIM, A_KV_HEADS * A_HEAD_DIM,
            IDX_HEADS * IDX_DIM, IDX_DIM, IDX_HEADS)

B_HEADS = 32
B_KV_HEADS = 4
B_HEAD_DIM = 64
B_ROT = B_HEAD_DIM // 4
WINDOW = 128
B_SPLITS = (B_HEADS * B_HEAD_DIM, B_KV_HEADS * B_HEAD_DIM, B_KV_HEADS * B_HEAD_DIM)

LANES = 128
MXU_COLS = 256
VMEM_LIMIT = 56 * 1024 * 1024

FFN_TM = 704
FFN_TF = 512
PROJ_TM = 384
ATT_TQ = 256
ATT_TK = 512
SWA_TQ = 256
RED_ROWS = 32

INT_MIN = -2 ** 31
NEG_MASK = -2e30
NEG_INIT = -1e30
LOG2_E = math.log2(math.e)


def _rms(x, g):
    return x * lax.rsqrt(jnp.mean(x * x, axis=-1, keepdims=True) + NORM_EPS) * g


def _ffn_kernel(x_ref, g_ref, wg_ref, wu_ref, wd_ref, gf_ref, o_ref, h_ref, *, final_norm):
    j = pl.program_id(1)

    @pl.when(j == 0)
    def _():
        h_ref[...] = _rms(x_ref[...], g_ref[...]).astype(BF16)
        o_ref[...] = jnp.zeros_like(o_ref)

    h = h_ref[...]
    g = jnp.dot(h, wg_ref[...], preferred_element_type=F32)
    u = jnp.dot(h, wu_ref[...], preferred_element_type=F32)
    a = (g * jax.nn.sigmoid(g) * u).astype(BF16)
    o_ref[...] += jnp.dot(a, wd_ref[...], preferred_element_type=F32)

    @pl.when(j == pl.num_programs(1) - 1)
    def _():
        y = x_ref[...] + 0.5 * o_ref[...]
        if final_norm:
            y = _rms(y, gf_ref[...])
        o_ref[...] = y


def _ffn(x, gain, wg, wu, wd, layer, final_gain=None):
    t = x.shape[0]
    final_norm = final_gain is not None
    gf = final_gain if final_norm else gain
    return pl.pallas_call(
        functools.partial(_ffn_kernel, final_norm=final_norm),
        grid=(t // FFN_TM, FFN_DIM // FFN_TF),
        in_specs=[
            pl.BlockSpec((FFN_TM, D_MODEL), lambda i, j: (i, 0)),
            pl.BlockSpec((1, D_MODEL), lambda i, j: (0, 0)),
            pl.BlockSpec((None, D_MODEL, FFN_TF), lambda i, j: (layer, 0, j)),
            pl.BlockSpec((None, D_MODEL, FFN_TF), lambda i, j: (layer, 0, j)),
            pl.BlockSpec((None, FFN_TF, D_MODEL), lambda i, j: (layer, j, 0)),
            pl.BlockSpec((1, D_MODEL), lambda i, j: (0, 0)),
        ],
        out_specs=pl.BlockSpec((FFN_TM, D_MODEL), lambda i, j: (i, 0)),
        out_shape=jax.ShapeDtypeStruct((t, D_MODEL), F32),
        scratch_shapes=[pltpu.VMEM((FFN_TM, D_MODEL), BF16)],
        compiler_params=pltpu.CompilerParams(
            dimension_semantics=("parallel", "arbitrary"), vmem_limit_bytes=VMEM_LIMIT),
        name="ffn",
    )(x, gain.reshape(1, D_MODEL), wg, wu, wd, gf.reshape(1, D_MODEL))


def _rope_tables(pos, rot, head_dim):
    half = rot // 2
    inv_freq = jnp.float32(ROPE_THETA) ** (-jnp.arange(half, dtype=F32) / half)
    d = np.arange(LANES) % head_dim
    ang = pos.astype(F32)[:, None] * inv_freq[d % half][None, :]
    c = jnp.where(d < rot, jnp.cos(ang), 1.0)
    s = jnp.where(d < half, -jnp.sin(ang), jnp.where(d < rot, jnp.sin(ang), 0.0))
    return jnp.stack([c, s])


def _proj_kernel(*refs, segs, half, head_dim, has_bias, has_ln):
    it = iter(refs)
    x_ref, g_ref, w_ref, tab_ref = next(it), next(it), next(it), next(it)
    b_ref = next(it) if has_bias else None
    lng_ref, lnb_ref = (next(it), next(it)) if has_ln else (None, None)
    out_refs = list(it)

    h = _rms(x_ref[...], g_ref[...]).astype(BF16)
    c_tab, s_tab = tab_ref[0], tab_ref[1]
    low_half = lax.broadcasted_iota(I32, c_tab.shape, 1) % head_dim < half

    for col0, ncols, rope, ln, scale, natural, transposed in segs:
        for b0 in range(0, ncols, MXU_COLS):
            bw = min(MXU_COLS, ncols - b0)
            y = jnp.dot(h, w_ref[:, col0 + b0:col0 + b0 + bw], preferred_element_type=F32)
            if has_bias:
                y = y + b_ref[:, col0 + b0:col0 + b0 + bw]
            for s0 in range(0, bw, LANES):
                ys = y[:, s0:s0 + LANES]
                if ln:
                    mu = jnp.mean(ys, axis=-1, keepdims=True)
                    yc = ys - mu
                    var = jnp.mean(yc * yc, axis=-1, keepdims=True)
                    ys = yc * lax.rsqrt(var + NORM_EPS) * lng_ref[...] + lnb_ref[...]
                if rope:
                    rot_half = jnp.where(low_half, pltpu.roll(ys, LANES - half, 1), pltpu.roll(ys, half, 1))
                    ys = ys * c_tab + rot_half * s_tab
                if scale is not None:
                    ys = ys * scale
                c = b0 + s0
                for o in natural:
                    out_refs[o][:, c:c + LANES] = ys.astype(out_refs[o].dtype)
                if transposed:
                    yt = ys.T
                    for o in transposed:
                        out_refs[o][c:c + LANES, :] = yt.astype(out_refs[o].dtype)


def _proj(x, gain, w, tab, segs, outs, half, head_dim, bias=None, ln=None):
    t = x.shape[0]
    wcols = w.shape[1]
    in_specs = [
        pl.BlockSpec((PROJ_TM, D_MODEL), lambda i: (i, 0)),
        pl.BlockSpec((1, D_MODEL), lambda i: (0, 0)),
        pl.BlockSpec((D_MODEL, wcols), lambda i: (0, 0), pipeline_mode=pl.Buffered(1)),
        pl.BlockSpec((2, PROJ_TM, LANES), lambda i: (0, i, 0)),
    ]
    args = [x, gain.reshape(1, D_MODEL), w, tab]
    if bias is not None:
        in_specs.append(pl.BlockSpec((1, wcols), lambda i: (0, 0)))
        args.append(bias.reshape(1, wcols))
    if ln is not None:
        in_specs += [pl.BlockSpec((1, LANES), lambda i: (0, 0))] * 2
        args += [ln[0].reshape(1, LANES), ln[1].reshape(1, LANES)]
    out_specs, out_shape = [], []
    for ncols, dtype, transposed in outs:
        if transposed:
            out_specs.append(pl.BlockSpec((ncols, PROJ_TM), lambda i: (0, i)))
            out_shape.append(jax.ShapeDtypeStruct((ncols, t), dtype))
        else:
            out_specs.append(pl.BlockSpec((PROJ_TM, ncols), lambda i: (i, 0)))
            out_shape.append(jax.ShapeDtypeStruct((t, ncols), dtype))
    return pl.pallas_call(
        functools.partial(_proj_kernel, segs=segs, half=half, head_dim=head_dim,
                          has_bias=bias is not None, has_ln=ln is not None),
        grid=(t // PROJ_TM,),
        in_specs=in_specs,
        out_specs=out_specs,
        out_shape=out_shape,
        compiler_params=pltpu.CompilerParams(
            dimension_semantics=("parallel",), vmem_limit_bytes=VMEM_LIMIT),
        name="proj",
    )(*args)


def _tdot(a, b):
    return lax.dot_general(a, b, (((0,), (0,)), ((), ())), preferred_element_type=F32)


def _outproj_kernel(x_ref, ot_ref, os_ref, w_ref, y_ref, *, n_prompt_tiles):
    i = pl.program_id(0)

    @pl.when(i < n_prompt_tiles)
    def _():
        y_ref[...] = x_ref[...] + _tdot(ot_ref[...], w_ref[...])

    @pl.when(i >= n_prompt_tiles)
    def _():
        y_ref[...] = x_ref[...] + jnp.dot(os_ref[...], w_ref[...], preferred_element_type=F32)


def _outproj(x, ot_prompt, o_sample, w):
    t = x.shape[0]
    n_s = o_sample.shape[0]
    n_p = SEQ // n_s
    return pl.pallas_call(
        functools.partial(_outproj_kernel, n_prompt_tiles=n_p),
        grid=(t // n_s,),
        in_specs=[
            pl.BlockSpec((n_s, D_MODEL), lambda i: (i, 0)),
            pl.BlockSpec((ot_prompt.shape[0], n_s), lambda i: (0, jnp.minimum(i, n_p - 1))),
            pl.BlockSpec(o_sample.shape, lambda i: (0, 0)),
            pl.BlockSpec(w.shape, lambda i: (0, 0)),
        ],
        out_specs=pl.BlockSpec((n_s, D_MODEL), lambda i: (i, 0)),
        out_shape=jax.ShapeDtypeStruct((t, D_MODEL), F32),
        compiler_params=pltpu.CompilerParams(
            dimension_semantics=("parallel",), vmem_limit_bytes=VMEM_LIMIT),
        name="outproj",
    )(x, ot_prompt, o_sample, w)


def _key_to_float(key):
    bits = jnp.where(key >= 0, key, key ^ jnp.int32(0x7FFFFFFF))
    exp_mask = jnp.int32(0x7F800000)
    bits = jnp.where((bits & exp_mask) == exp_mask, bits & jnp.int32(-0x800000), bits)
    return pltpu.bitcast(bits, F32)


def _col_reduce(x, op):
    r, n = x.shape
    return op(op(x.reshape(r // RED_ROWS, RED_ROWS, n), axis=0), axis=0, keepdims=True)


def _dsa_attn_kernel(qit_ref, qt_ref, wt_ref, ki_ref, k_ref, v_ref, ot_ref,
                     s_ref, bias_ref, m_ref, l_ref, acc_ref,
                     *, tq, tk, prompt, n_chunks_static, n_valid, topk):
    j = pl.program_id(0)
    sub = tk // LANES
    qlane = lax.broadcasted_iota(I32, (1, tq), 1)
    if prompt:
        limit = ((j * tq + qlane) // CHUNK + 1) * CHUNK
        n_chunks = (j * tq + tq + tk - 1) // tk
    else:
        limit = jnp.full((1, tq), n_valid, I32)
        n_chunks = n_chunks_static
    krow = lax.broadcasted_iota(I32, (LANES, tq), 0)

    def head(ref, h):
        return ref[h * LANES:(h + 1) * LANES, :]

    def score_chunk(c, carry):
        for s in range(sub):
            k0 = pl.multiple_of(c * tk + s * LANES, LANES)
            kis = ki_ref[pl.ds(k0, LANES), :]
            acc = jnp.zeros((LANES, tq), F32)
            for h in range(IDX_HEADS):
                lg = jnp.dot(kis, head(qit_ref, h), preferred_element_type=F32)
                acc = acc + wt_ref[h:h + 1, :] * jnp.maximum(lg, 0.0)
            s_ref[c, s * LANES:(s + 1) * LANES, :] = jnp.where(k0 + krow < limit, acc, -jnp.inf)
        return carry

    lax.fori_loop(0, n_chunks, score_chunk, 0)

    def bisect(it, prefix):
        bit = lax.shift_left(jnp.int32(1), jnp.int32(31) - it)
        cand_u = prefix | bit
        cand_f = _key_to_float(cand_u ^ jnp.int32(INT_MIN))

        def count_chunk(c, acc):
            ones = jnp.where(s_ref[c] >= cand_f, 1.0, 0.0)
            return acc + jnp.sum(ones.reshape(tk // RED_ROWS, RED_ROWS, tq), axis=0)

        acc = lax.fori_loop(0, n_chunks, count_chunk, jnp.zeros((RED_ROWS, tq), F32))
        cnt = jnp.sum(acc, axis=0, keepdims=True)
        return jnp.where(cnt >= float(topk), cand_u, prefix)

    prefix = lax.fori_loop(0, 32, bisect, jnp.zeros((1, tq), I32))
    tau_f = _key_to_float(jnp.maximum(prefix ^ jnp.int32(INT_MIN), jnp.int32(INT_MIN + 0x00800000)))

    m_ref[...] = jnp.full(m_ref.shape, NEG_INIT, F32)
    l_ref[...] = jnp.zeros(l_ref.shape, F32)
    acc_ref[...] = jnp.zeros(acc_ref.shape, F32)
    c_exp = A_HEAD_DIM ** -0.5 * LOG2_E

    def attn_chunk(c, carry):
        bias_ref[...] = jnp.where(s_ref[c] >= tau_f, 0.0, NEG_MASK)
        m_old = [m_ref[h] for h in range(A_HEADS)]
        l_old = [l_ref[h] for h in range(A_HEADS)]
        m_new, l_new = [], []

        keys = pl.ds(pl.multiple_of(c * tk, tk), tk)

        def kv_head(ref, h):
            return ref[keys, (h // A_GROUP) * A_HEAD_DIM:(h // A_GROUP + 1) * A_HEAD_DIM]

        def logits(h):
            return jnp.dot(kv_head(k_ref, h), head(qt_ref, h), preferred_element_type=F32) + bias_ref[...]

        sb = logits(0)
        for h in range(A_HEADS):
            sb_next = logits(h + 1) if h + 1 < A_HEADS else None
            mn = jnp.maximum(m_old[h], _col_reduce(sb, jnp.max))
            p = jnp.exp2((sb - mn) * c_exp)
            alpha = jnp.exp2((m_old[h] - mn) * c_exp)
            m_new.append(mn)
            l_new.append(alpha * l_old[h] + _col_reduce(p, jnp.sum))
            acc_ref[h] = alpha * acc_ref[h] + _tdot(kv_head(v_ref, h), p.astype(BF16))
            sb = sb_next
        for h in range(A_HEADS):
            m_ref[h] = m_new[h]
            l_ref[h] = l_new[h]
        return carry

    lax.fori_loop(0, n_chunks, attn_chunk, 0)

    for h in range(A_HEADS):
        ot_ref[h * LANES:(h + 1) * LANES, :] = (acc_ref[h] / l_ref[h]).astype(ot_ref.dtype)


def _dsa_attn(qit, qt, wt, ki, k, v, *, tq, n_keys, prompt, n_valid, topk):
    tk = ATT_TK
    nc = n_keys // tk
    if prompt:
        nt = SEQ // tq
        qmap = lambda j: (0, 0, j)
        kmap = lambda j: (0, 0, 0)
        mode = dict(pipeline_mode=pl.Buffered(1))
    else:
        nt = qt.shape[0]
        qmap = lambda j: (j, 0, 0)
        kmap = lambda j: (j, 0, 0)
        mode = {}
    hd = A_HEADS * A_HEAD_DIM
    kvd = A_KV_HEADS * A_HEAD_DIM
    return pl.pallas_call(
        functools.partial(_dsa_attn_kernel, tq=tq, tk=tk, prompt=prompt, n_chunks_static=nc,
                          n_valid=n_valid, topk=topk),
        grid=(nt,),
        in_specs=[
            pl.BlockSpec((None, IDX_HEADS * IDX_DIM, tq), qmap),
            pl.BlockSpec((None, hd, tq), qmap),
            pl.BlockSpec((None, IDX_HEADS, tq), qmap),
            pl.BlockSpec((None, n_keys, IDX_DIM), kmap, **mode),
            pl.BlockSpec((None, n_keys, kvd), kmap, **mode),
            pl.BlockSpec((None, n_keys, kvd), kmap, **mode),
        ],
        out_specs=pl.BlockSpec((None, hd, tq), qmap),
        out_shape=jax.ShapeDtypeStruct((qt.shape[0], hd, nt * tq if prompt else tq), BF16),
        scratch_shapes=[
            pltpu.VMEM((nc, tk, tq), F32),
            pltpu.VMEM((tk, tq), F32),
            pltpu.VMEM((A_HEADS, 1, tq), F32),
            pltpu.VMEM((A_HEADS, 1, tq), F32),
            pltpu.VMEM((A_HEADS, A_HEAD_DIM, tq), F32),
        ],
        compiler_params=pltpu.CompilerParams(
            dimension_semantics=("parallel",), vmem_limit_bytes=VMEM_LIMIT),
        name="dsa_attn_prompt" if prompt else "dsa_attn_sample",
    )(qit, qt, wt, ki, k, v)


def _sample_queries_t(at):
    c = at.shape[0]
    a = at[:, SEQ:].reshape(c, DEC_BATCH, DEC_SEQ).transpose(1, 0, 2)
    return jnp.pad(a, ((0, 0), (0, 0), (0, LANES - DEC_SEQ)))


def _sample_rows(ot):
    b, c, _ = ot.shape
    return ot[..., :DEC_SEQ].transpose(0, 2, 1).reshape(b * DEC_SEQ, c)


def _swa_attn_kernel(sink_ref, qt_ref, kp_ref, kc_ref, vp_ref, vc_ref, ot_ref, *, tq, prompt, n_valid):
    j = pl.program_id(0)
    kb = jnp.concatenate([kp_ref[...], kc_ref[...]], axis=0)
    vb = jnp.concatenate([vp_ref[...], vc_ref[...]], axis=0)
    nk = kb.shape[0]
    krow = lax.broadcasted_iota(I32, (nk, tq), 0)
    if prompt:
        kchunk = krow // CHUNK
        qchunk = lax.broadcasted_iota(I32, (nk, tq), 1) // CHUNK
        wc = WINDOW // CHUNK
        mask = (kchunk >= qchunk) & (kchunk <= qchunk + wc) & ((j > 0) | (kchunk >= wc))
    else:
        mask = krow < n_valid
    c_exp = B_HEAD_DIM ** -0.5 * LOG2_E
    bias = jnp.where(mask, 0.0, NEG_MASK)
    zeros = jnp.zeros((B_HEAD_DIM, tq), BF16)
    group = B_HEADS // B_KV_HEADS

    def logits(h):
        g = h // group
        qh = qt_ref[h * B_HEAD_DIM:(h + 1) * B_HEAD_DIM, :]
        qpad = jnp.concatenate([qh, zeros] if g % 2 == 0 else [zeros, qh], axis=0)
        kblk = kb[:, (g // 2) * LANES:(g // 2 + 1) * LANES]
        return jnp.dot(kblk, qpad, preferred_element_type=F32)

    s = logits(0)
    for h in range(B_HEADS):
        s_next = logits(h + 1) if h + 1 < B_HEADS else None
        g = h // group
        s2 = s * c_exp + bias
        sink2 = sink_ref[h] * LOG2_E
        m2 = jnp.maximum(_col_reduce(s2, jnp.max), sink2)
        p = jnp.exp2(s2 - m2)
        den = _col_reduce(p, jnp.sum) + jnp.exp2(sink2 - m2)
        pn = (p * (1.0 / den)).astype(BF16)
        o2 = _tdot(vb[:, (g // 2) * LANES:(g // 2 + 1) * LANES], pn)
        oh = o2[:B_HEAD_DIM] if g % 2 == 0 else o2[B_HEAD_DIM:]
        ot_ref[h * B_HEAD_DIM:(h + 1) * B_HEAD_DIM, :] = oh.astype(ot_ref.dtype)
        s = s_next


def _swa_attn(sinks, qt, k, v, *, tq, prompt, n_valid):
    hd = B_HEADS * B_HEAD_DIM
    kvd = B_KV_HEADS * B_HEAD_DIM
    if prompt:
        nt = SEQ // tq
        per = tq // WINDOW
        qmap = lambda j: (0, 0, j)
        prev = lambda j: (0, jnp.maximum(per * j - 1, 0), 0)
        cur = lambda j: (0, j, 0)
        rc = tq
    else:
        nt = qt.shape[0]
        qmap = lambda j: (j, 0, 0)
        prev = lambda j: (j, 0, 0)
        cur = lambda j: (j, 1, 0)
        rc = WINDOW
    return pl.pallas_call(
        functools.partial(_swa_attn_kernel, tq=tq, prompt=prompt, n_valid=n_valid),
        grid=(nt,),
        in_specs=[
            pl.BlockSpec(memory_space=pltpu.SMEM),
            pl.BlockSpec((None, hd, tq), qmap),
            pl.BlockSpec((None, WINDOW, kvd), prev),
            pl.BlockSpec((None, rc, kvd), cur),
            pl.BlockSpec((None, WINDOW, kvd), prev),
            pl.BlockSpec((None, rc, kvd), cur),
        ],
        out_specs=pl.BlockSpec((None, hd, tq), qmap),
        out_shape=jax.ShapeDtypeStruct((qt.shape[0], hd, nt * tq if prompt else tq), BF16),
        compiler_params=pltpu.CompilerParams(
            dimension_semantics=("parallel",), vmem_limit_bytes=VMEM_LIMIT),
        name="swa_attn_prompt" if prompt else "swa_attn_sample",
    )(sinks, qt, k, k, v, v)


def kernel(x_prompt, x_sample, cache_dsa_k, cache_dsa_v, cache_dsa_kidx, cache_swa_k, cache_swa_v,
           norm_ffn1, ffn1_gate, ffn1_up, ffn1_down, norm_mix,
           dsa_w_in, dsa_kln_g, dsa_kln_b, dsa_w_out,
           swa_w_in, swa_b_in, swa_sinks, swa_w_out,
           norm_ffn2, ffn2_gate, ffn2_up, ffn2_down, norm_final):
    n_s = DEC_BATCH * DEC_SEQ
    x = jnp.concatenate([x_prompt.reshape(SEQ, D_MODEL), x_sample.reshape(n_s, D_MODEL)], axis=0)
    pos = jnp.concatenate([jnp.arange(SEQ), PAST_LEN + jnp.tile(jnp.arange(DEC_SEQ), DEC_BATCH)])

    g1, u1, d1 = ffn1_gate.astype(BF16), ffn1_up.astype(BF16), ffn1_down.astype(BF16)
    g2, u2, d2 = ffn2_gate.astype(BF16), ffn2_up.astype(BF16), ffn2_down.astype(BF16)

    x = _ffn(x, norm_ffn1[0], g1, u1, d1, 0)

    a_in = sum(A_SPLITS)
    a_pad = -a_in % LANES
    w_in = jnp.pad(dsa_w_in[0], ((0, 0), (0, a_pad))).astype(BF16)
    offs = np.cumsum((0,) + A_SPLITS)
    w_scale = IDX_HEADS ** -0.5 * IDX_DIM ** -0.5
    segs = (
        (int(offs[0]), A_SPLITS[0], True, False, None, (), (0,)),
        (int(offs[1]), A_SPLITS[1], True, False, None, (1, 2), ()),
        (int(offs[2]), A_SPLITS[2], False, False, None, (3, 4), ()),
        (int(offs[3]), A_SPLITS[3], True, False, None, (), (5,)),
        (int(offs[4]), A_SPLITS[4], True, True, None, (6, 7), ()),
        (int(offs[5]), LANES, False, False, w_scale, (), (8,)),
    )
    outs = [(A_SPLITS[0], BF16, True), (A_SPLITS[1], F32, False), (A_SPLITS[1], BF16, False),
            (A_SPLITS[2], F32, False), (A_SPLITS[2], BF16, False), (A_SPLITS[3], BF16, True),
            (A_SPLITS[4], F32, False), (A_SPLITS[4], BF16, False), (LANES, F32, True)]
    qt_a, k_a, k_a16, v_a, v_a16, qit_a, ki_a, ki_a16, wt_a = _proj(
        x, norm_mix[0], w_in, _rope_tables(pos, A_ROT, A_HEAD_DIM), segs, outs, A_ROT // 2, A_HEAD_DIM,
        ln=(dsa_kln_g[0], dsa_kln_b[0]))

    ot_p = _dsa_attn(qit_a[None], qt_a[None], wt_a[None], ki_a16[None], k_a16[None], v_a16[None],
                     tq=ATT_TQ, n_keys=SEQ, prompt=True, n_valid=SEQ, topk=min(TOPK_MAX, SEQ // 4))

    l_s = PAST_LEN + DEC_SEQ
    n_keys_s = -(-l_s // ATT_TK) * ATT_TK

    def with_cache(cache, new):
        new = new[SEQ:].reshape(DEC_BATCH, DEC_SEQ, -1)
        cache = cache.reshape(DEC_BATCH, PAST_LEN, -1).astype(BF16)
        return jnp.pad(jnp.concatenate([cache, new], axis=1), ((0, 0), (0, n_keys_s - l_s), (0, 0)))

    ot_s = _dsa_attn(_sample_queries_t(qit_a), _sample_queries_t(qt_a), _sample_queries_t(wt_a[:IDX_HEADS]),
                     with_cache(cache_dsa_kidx[0], ki_a16), with_cache(cache_dsa_k[0], k_a16),
                     with_cache(cache_dsa_v[0], v_a16),
                     tq=LANES, n_keys=n_keys_s, prompt=False, n_valid=l_s, topk=min(TOPK_MAX, l_s // 4))

    x = _outproj(x, ot_p[0], _sample_rows(ot_s), dsa_w_out[0].astype(BF16))
    x = _ffn(x, norm_ffn2[0], g2, u2, d2, 0)

    x = _ffn(x, norm_ffn1[1], g1, u1, d1, 1)

    boffs = np.cumsum((0,) + B_SPLITS)
    bsegs = (
        (int(boffs[0]), B_SPLITS[0], True, False, None, (), (0,)),
        (int(boffs[1]), B_SPLITS[1], True, False, None, (1, 2), ()),
        (int(boffs[2]), B_SPLITS[2], False, False, None, (3, 4), ()),
    )
    bouts = [(B_SPLITS[0], BF16, True), (B_SPLITS[1], F32, False), (B_SPLITS[1], BF16, False),
             (B_SPLITS[2], F32, False), (B_SPLITS[2], BF16, False)]
    qt_b, k_b, k_b16, v_b, v_b16 = _proj(
        x, norm_mix[1], swa_w_in[0].astype(BF16), _rope_tables(pos, B_ROT, B_HEAD_DIM),
        bsegs, bouts, B_ROT // 2, B_HEAD_DIM, bias=swa_b_in[0])

    sinks = swa_sinks[0]
    ot_bp = _swa_attn(sinks, qt_b[None], k_b16[None], v_b16[None], tq=SWA_TQ, prompt=True, n_valid=0)

    rows = cache_swa_k.shape[2]

    def swa_keys(cache, new):
        new = new[SEQ:].reshape(DEC_BATCH, DEC_SEQ, -1)
        cache = cache.reshape(DEC_BATCH, rows, -1).astype(BF16)
        return jnp.pad(jnp.concatenate([cache, new], axis=1),
                       ((0, 0), (0, 2 * WINDOW - rows - DEC_SEQ), (0, 0)))

    ot_bs = _swa_attn(sinks, _sample_queries_t(qt_b), swa_keys(cache_swa_k[0], k_b16),
                      swa_keys(cache_swa_v[0], v_b16), tq=LANES, prompt=False, n_valid=rows + DEC_SEQ)

    x = _outproj(x, ot_bp[0], _sample_rows(ot_bs), swa_w_out[0].astype(BF16))
    y = _ffn(x, norm_ffn2[1], g2, u2, d2, 1, final_gain=norm_final)

    y_prompt = y[:SEQ].reshape(1, SEQ, D_MODEL)
    y_sample = y[SEQ:].reshape(DEC_BATCH, DEC_SEQ, D_MODEL)

    def split(a, *tail):
        return (a[:SEQ].reshape((1, 1, SEQ) + tail), a[SEQ:].reshape((1, DEC_BATCH, DEC_SEQ) + tail))

    dkp, dks = split(k_a, A_KV_HEADS, A_HEAD_DIM)
    dvp, dvs = split(v_a, A_KV_HEADS, A_HEAD_DIM)
    dip, dis = split(ki_a, IDX_DIM)
    skp, sks = split(k_b, B_KV_HEADS, B_HEAD_DIM)
    svp, svs = split(v_b, B_KV_HEADS, B_HEAD_DIM)
    wrows = min(WINDOW, SEQ)
    return (y_prompt, y_sample, dkp, dvp, dip, skp[:, :, SEQ - wrows:], svp[:, :, SEQ - wrows:],
            dks, dvs, dis, sks, svs)
```

```python
import functools
import math

import jax
import jax.numpy as jnp
import numpy as np
from jax import lax
from jax.experimental import pallas as pl
from jax.experimental.pallas import tpu as pltpu

F32 = jnp.float32
BF16 = jnp.bfloat16
I32 = jnp.int32

D_MODEL = 2048
SEQ = 8192
DEC_BATCH = 8
DEC_SEQ = 32
PAST_LEN = 1024
N_TOK = SEQ + DEC_BATCH * DEC_SEQ
CHUNK = 64
ROPE_THETA = 500000.0
NORM_EPS = 1e-6
FFN_DIM = 5632

A_HEADS = 16
A_KV_HEADS = 4
A_GROUP = A_HEADS // A_KV_HEADS
A_HEAD_DIM = 128
A_ROT = A_HEAD_DIM // 4
IDX_HEADS = 16
IDX_DIM = 128
TOPK_MAX = 256
A_SPLITS = (A_HEADS * A_HEAD_DIM, A_KV_HEADS * A_HEAD_DIM, A_KV_HEADS * A_HEAD_DIM,
            IDX_HEADS * IDX_DIM, IDX_DIM, IDX_HEADS)

B_HEADS = 32
B_KV_HEADS = 4
B_HEAD_DIM = 64
B_ROT = B_HEAD_DIM // 4
WINDOW = 128
B_SPLITS = (B_HEADS * B_HEAD_DIM, B_KV_HEADS * B_HEAD_DIM, B_KV_HEADS * B_HEAD_DIM)

LANES = 128
MXU_COLS = 256
BF16_SUBLANES = 16
VMEM_LIMIT = 56 * 1024 * 1024

FFN_TM = 704
FFN_TF = 512
PROJ_TM = 384
ATT_TQ = 256
ATT_TK = 512
SWA_TQ = 256
LOOKAHEAD = 4
RED_ROWS = 32

INT_MIN = -2 ** 31
NEG_MASK = -2e30
NEG_INIT = -1e30
LOG2_E = math.log2(math.e)


def _rms(x, g):
    return x * lax.rsqrt(jnp.mean(x * x, axis=-1, keepdims=True) + NORM_EPS) * g


def _ffn_kernel(*refs, final_norm, split_in, split_out):
    it = iter(refs)
    x_ref = next(it)
    xs_ref = next(it) if split_in else None
    g_ref, wg_ref, wu_ref, wd_ref, gf_ref, o_ref = (next(it) for _ in range(6))
    os_ref = next(it) if split_out else None
    h_ref = next(it)
    xbuf_ref = next(it) if split_in else x_ref
    i, j = pl.program_id(0), pl.program_id(1)
    last_tile = i == pl.num_programs(0) - 1
    n_s = DEC_BATCH * DEC_SEQ
    n_p = FFN_TM - n_s

    @pl.when(j == 0)
    def _():
        if split_in:
            @pl.when(jnp.logical_not(last_tile))
            def _():
                xbuf_ref[...] = x_ref[...]

            @pl.when(last_tile)
            def _():
                xbuf_ref[:n_p] = x_ref[:n_p]
                xbuf_ref[n_p:] = xs_ref[...]
        h_ref[...] = _rms(xbuf_ref[...], g_ref[...]).astype(BF16)
        o_ref[...] = jnp.zeros_like(o_ref)

    h = h_ref[...]
    g = jnp.dot(h, wg_ref[...], preferred_element_type=F32)
    u = jnp.dot(h, wu_ref[...], preferred_element_type=F32)
    a = (g * jax.nn.sigmoid(g) * u).astype(BF16)
    o_ref[...] += jnp.dot(a, wd_ref[...], preferred_element_type=F32)

    @pl.when(j == pl.num_programs(1) - 1)
    def _():
        y = xbuf_ref[...] + 0.5 * o_ref[...]
        if final_norm:
            y = _rms(y, gf_ref[...])
        o_ref[...] = y
        if split_out:
            @pl.when(last_tile)
            def _():
                os_ref[...] = y[n_p:]


def _ffn(x, gain, wg, wu, wd, layer, final_gain=None, x_sample=None, split_out=False):
    split_in = x_sample is not None
    n_s = DEC_BATCH * DEC_SEQ
    final_norm = final_gain is not None
    gf = final_gain if final_norm else gain
    row_spec = pl.BlockSpec((FFN_TM, D_MODEL), lambda i, j: (i, 0))
    sample_spec = pl.BlockSpec((n_s, D_MODEL), lambda i, j: (0, 0))
    vec_spec = pl.BlockSpec((1, D_MODEL), lambda i, j: (0, 0))
    in_specs = [row_spec] + ([sample_spec] if split_in else []) + [
        vec_spec,
        pl.BlockSpec((None, D_MODEL, FFN_TF), lambda i, j: (layer, 0, j)),
        pl.BlockSpec((None, D_MODEL, FFN_TF), lambda i, j: (layer, 0, j)),
        pl.BlockSpec((None, FFN_TF, D_MODEL), lambda i, j: (layer, j, 0)),
        vec_spec,
    ]
    args = [x] + ([x_sample] if split_in else []) + [gain.reshape(1, D_MODEL), wg, wu, wd, gf.reshape(1, D_MODEL)]
    if split_out:
        out_specs = [row_spec, sample_spec]
        out_shape = [jax.ShapeDtypeStruct((SEQ, D_MODEL), F32), jax.ShapeDtypeStruct((n_s, D_MODEL), F32)]
    else:
        out_specs = row_spec
        out_shape = jax.ShapeDtypeStruct((N_TOK, D_MODEL), F32)
    scratch = [pltpu.VMEM((FFN_TM, D_MODEL), BF16)] + ([pltpu.VMEM((FFN_TM, D_MODEL), F32)] if split_in else [])
    return pl.pallas_call(
        functools.partial(_ffn_kernel, final_norm=final_norm, split_in=split_in, split_out=split_out),
        grid=(N_TOK // FFN_TM, FFN_DIM // FFN_TF),
        in_specs=in_specs,
        out_specs=out_specs,
        out_shape=out_shape,
        scratch_shapes=scratch,
        compiler_params=pltpu.CompilerParams(
            dimension_semantics=("arbitrary" if split_out else "parallel", "arbitrary"),
            vmem_limit_bytes=VMEM_LIMIT),
        name="ffn",
    )(*args)


def _rope_tables(pos, rot, head_dim):
    half = rot // 2
    inv_freq = jnp.float32(ROPE_THETA) ** (-jnp.arange(half, dtype=F32) / half)
    d = np.arange(LANES) % head_dim
    ang = pos.astype(F32)[:, None] * inv_freq[d % half][None, :]
    c = jnp.where(d < rot, jnp.cos(ang), 1.0)
    s = jnp.where(d < half, -jnp.sin(ang), jnp.where(d < rot, jnp.sin(ang), 0.0))
    return jnp.stack([c, s])


def _proj_kernel(*refs, segs, half, head_dim, has_bias, has_ln):
    it = iter(refs)
    x_ref, g_ref, w_ref, tab_ref = next(it), next(it), next(it), next(it)
    b_ref = next(it) if has_bias else None
    lng_ref, lnb_ref = (next(it), next(it)) if has_ln else (None, None)
    out_refs = list(it)

    h = _rms(x_ref[...], g_ref[...]).astype(BF16)
    c_tab, s_tab = tab_ref[0], tab_ref[1]
    low_half = lax.broadcasted_iota(I32, c_tab.shape, 1) % head_dim < half

    for col0, ncols, rope, ln, scale, natural, transposed in segs:
        for b0 in range(0, ncols, MXU_COLS):
            bw = min(MXU_COLS, ncols - b0)
            y = jnp.dot(h, w_ref[:, col0 + b0:col0 + b0 + bw], preferred_element_type=F32)
            if has_bias:
                y = y + b_ref[:, col0 + b0:col0 + b0 + bw]
            for s0 in range(0, bw, LANES):
                ys = y[:, s0:s0 + LANES]
                if ln:
                    mu = jnp.mean(ys, axis=-1, keepdims=True)
                    yc = ys - mu
                    var = jnp.mean(yc * yc, axis=-1, keepdims=True)
                    ys = yc * lax.rsqrt(var + NORM_EPS) * lng_ref[...] + lnb_ref[...]
                if rope:
                    rot_half = jnp.where(low_half, pltpu.roll(ys, LANES - half, 1), pltpu.roll(ys, half, 1))
                    ys = ys * c_tab + rot_half * s_tab
                if scale is not None:
                    ys = ys * scale
                c = b0 + s0
                for o in natural:
                    out_refs[o][:, c:c + LANES] = ys.astype(out_refs[o].dtype)
                if transposed:
                    yt = ys.T
                    for o in transposed:
                        out_refs[o][c:c + LANES, :] = yt.astype(out_refs[o].dtype)


def _proj(x, gain, w, tab, segs, outs, half, head_dim, bias=None, ln=None):
    t = x.shape[0]
    wcols = w.shape[1]
    in_specs = [
        pl.BlockSpec((PROJ_TM, D_MODEL), lambda i: (i, 0)),
        pl.BlockSpec((1, D_MODEL), lambda i: (0, 0)),
        pl.BlockSpec((D_MODEL, wcols), lambda i: (0, 0), pipeline_mode=pl.Buffered(1)),
        pl.BlockSpec((2, PROJ_TM, LANES), lambda i: (0, i, 0)),
    ]
    args = [x, gain.reshape(1, D_MODEL), w, tab]
    if bias is not None:
        in_specs.append(pl.BlockSpec((1, wcols), lambda i: (0, 0)))
        args.append(bias.reshape(1, wcols))
    if ln is not None:
        in_specs += [pl.BlockSpec((1, LANES), lambda i: (0, 0))] * 2
        args += [ln[0].reshape(1, LANES), ln[1].reshape(1, LANES)]
    out_specs, out_shape = [], []
    for ncols, dtype, transposed in outs:
        if transposed:
            out_specs.append(pl.BlockSpec((ncols, PROJ_TM), lambda i: (0, i)))
            out_shape.append(jax.ShapeDtypeStruct((ncols, t), dtype))
        else:
            out_specs.append(pl.BlockSpec((PROJ_TM, ncols), lambda i: (i, 0)))
            out_shape.append(jax.ShapeDtypeStruct((t, ncols), dtype))
    return pl.pallas_call(
        functools.partial(_proj_kernel, segs=segs, half=half, head_dim=head_dim,
                          has_bias=bias is not None, has_ln=ln is not None),
        grid=(t // PROJ_TM,),
        in_specs=in_specs,
        out_specs=out_specs,
        out_shape=out_shape,
        compiler_params=pltpu.CompilerParams(
            dimension_semantics=("parallel",), vmem_limit_bytes=VMEM_LIMIT),
        name="proj",
    )(*args)


def _tdot(a, b):
    return lax.dot_general(a, b, (((0,), (0,)), ((), ())), preferred_element_type=F32)


def _outproj_kernel(x_ref, ot_ref, os_ref, w_ref, y_ref, *, n_prompt_tiles):
    i = pl.program_id(0)

    @pl.when(i < n_prompt_tiles)
    def _():
        y_ref[...] = x_ref[...] + _tdot(ot_ref[...], w_ref[...])

    @pl.when(i >= n_prompt_tiles)
    def _():
        y_ref[...] = x_ref[...] + jnp.dot(os_ref[...], w_ref[...], preferred_element_type=F32)


def _outproj(x, ot_prompt, o_sample, w):
    t = x.shape[0]
    n_s = o_sample.shape[0]
    n_p = SEQ // n_s
    return pl.pallas_call(
        functools.partial(_outproj_kernel, n_prompt_tiles=n_p),
        grid=(t // n_s,),
        in_specs=[
            pl.BlockSpec((n_s, D_MODEL), lambda i: (i, 0)),
            pl.BlockSpec((ot_prompt.shape[0], n_s), lambda i: (0, jnp.minimum(i, n_p - 1))),
            pl.BlockSpec(o_sample.shape, lambda i: (0, 0)),
            pl.BlockSpec(w.shape, lambda i: (0, 0)),
        ],
        out_specs=pl.BlockSpec((n_s, D_MODEL), lambda i: (i, 0)),
        out_shape=jax.ShapeDtypeStruct((t, D_MODEL), F32),
        compiler_params=pltpu.CompilerParams(
            dimension_semantics=("parallel",), vmem_limit_bytes=VMEM_LIMIT),
        name="outproj",
    )(x, ot_prompt, o_sample, w)


def _key_to_float(key):
    bits = jnp.where(key >= 0, key, key ^ jnp.int32(0x7FFFFFFF))
    exp_mask = jnp.int32(0x7F800000)
    bits = jnp.where((bits & exp_mask) == exp_mask, bits & jnp.int32(-0x800000), bits)
    return pltpu.bitcast(bits, F32)


def _col_reduce(x, op):
    r, n = x.shape
    return op(op(x.reshape(r // RED_ROWS, RED_ROWS, n), axis=0), axis=0, keepdims=True)


def _dsa_attn_kernel(qit_ref, qt_ref, wt_ref, ki_ref, k_ref, v_ref, ot_ref,
                     s_ref, bias_ref, m_ref, l_ref, acc_ref,
                     *, tq, tk, prompt, n_chunks_static, n_valid, topk):
    j = pl.program_id(0)
    sub = tk // LANES
    qlane = lax.broadcasted_iota(I32, (1, tq), 1)
    if prompt:
        limit = ((j * tq + qlane) // CHUNK + 1) * CHUNK
        n_chunks = (j * tq + tq + tk - 1) // tk
    else:
        limit = jnp.full((1, tq), n_valid, I32)
        n_chunks = n_chunks_static
    krow = lax.broadcasted_iota(I32, (LANES, tq), 0)

    def head(ref, h):
        return ref[h * LANES:(h + 1) * LANES, :]

    def score_chunk(c, carry):
        for s in range(sub):
            k0 = pl.multiple_of(c * tk + s * LANES, LANES)
            kis = ki_ref[pl.ds(k0, LANES), :]
            acc = jnp.zeros((LANES, tq), F32)
            for h in range(IDX_HEADS):
                lg = jnp.dot(kis, head(qit_ref, h), preferred_element_type=F32)
                acc = acc + wt_ref[h:h + 1, :] * jnp.maximum(lg, 0.0)
            s_ref[c, s * LANES:(s + 1) * LANES, :] = jnp.where(k0 + krow < limit, acc, -jnp.inf)
        return carry

    lax.fori_loop(0, n_chunks, score_chunk, 0)

    ones_rows = jnp.ones((BF16_SUBLANES, tk), BF16)

    def bisect(it, prefix):
        bit = lax.shift_left(jnp.int32(1), jnp.int32(31) - it)
        cand_u = prefix | bit
        cand_f = _key_to_float(cand_u ^ jnp.int32(INT_MIN))

        def count_chunk(c, acc):
            ones = jnp.where(s_ref[c] >= cand_f, 1.0, 0.0)
            return acc + jnp.sum(ones.reshape(tk // RED_ROWS, RED_ROWS, tq), axis=0)

        acc = lax.fori_loop(0, n_chunks, count_chunk, jnp.zeros((RED_ROWS, tq), F32))
        cnt = jnp.sum(acc, axis=0, keepdims=True)
        return jnp.where(cnt >= float(topk), cand_u, prefix)

    prefix = lax.fori_loop(0, 32, bisect, jnp.zeros((1, tq), I32))
    tau_f = _key_to_float(jnp.maximum(prefix ^ jnp.int32(INT_MIN), jnp.int32(INT_MIN + 0x00800000)))

    m_ref[...] = jnp.full(m_ref.shape, NEG_INIT, F32)
    l_ref[...] = jnp.zeros(l_ref.shape, F32)
    acc_ref[...] = jnp.zeros(acc_ref.shape, F32)
    c_exp = A_HEAD_DIM ** -0.5 * LOG2_E

    def attn_chunk(c, carry):
        bias_ref[...] = jnp.where(s_ref[c] >= tau_f, 0.0, NEG_MASK)
        m_old = [m_ref[h] for h in range(A_HEADS)]
        l_old = [l_ref[h] for h in range(A_HEADS)]
        m_new, l_new = [], []

        keys = pl.ds(pl.multiple_of(c * tk, tk), tk)

        def kv_head(ref, h):
            return ref[keys, (h // A_GROUP) * A_HEAD_DIM:(h // A_GROUP + 1) * A_HEAD_DIM]

        def logits(h):
            return jnp.dot(kv_head(k_ref, h), head(qt_ref, h), preferred_element_type=F32) + bias_ref[...]

        ahead = [logits(h) for h in range(LOOKAHEAD)]
        for h in range(A_HEADS):
            if h + LOOKAHEAD < A_HEADS:
                ahead.append(logits(h + LOOKAHEAD))
            sb = ahead.pop(0)
            mn = jnp.maximum(m_old[h], _col_reduce(sb, jnp.max))
            p = jnp.exp2((sb - mn) * c_exp)
            alpha = jnp.exp2((m_old[h] - mn) * c_exp)
            m_new.append(mn)
            pb = p.astype(BF16)
            l_new.append(alpha * l_old[h] + jnp.dot(ones_rows, pb, preferred_element_type=F32)[:1])
            acc_ref[h] = alpha * acc_ref[h] + _tdot(kv_head(v_ref, h), pb)
        for h in range(A_HEADS):
            m_ref[h] = m_new[h]
            l_ref[h] = l_new[h]
        return carry

    lax.fori_loop(0, n_chunks, attn_chunk, 0)

    for h in range(A_HEADS):
        ot_ref[h * LANES:(h + 1) * LANES, :] = (acc_ref[h] / l_ref[h]).astype(ot_ref.dtype)


def _dsa_attn(qit, qt, wt, ki, k, v, *, tq, n_keys, prompt, n_valid, topk):
    tk = ATT_TK
    nc = n_keys // tk
    if prompt:
        nt = SEQ // tq
        qmap = lambda j: (0, 0, j)
        kmap = lambda j: (0, 0, 0)
        mode = dict(pipeline_mode=pl.Buffered(1))
    else:
        nt = qt.shape[0]
        qmap = lambda j: (j, 0, 0)
        kmap = lambda j: (j, 0, 0)
        mode = {}
    hd = A_HEADS * A_HEAD_DIM
    kvd = A_KV_HEADS * A_HEAD_DIM
    return pl.pallas_call(
        functools.partial(_dsa_attn_kernel, tq=tq, tk=tk, prompt=prompt, n_chunks_static=nc,
                          n_valid=n_valid, topk=topk),
        grid=(nt,),
        in_specs=[
            pl.BlockSpec((None, IDX_HEADS * IDX_DIM, tq), qmap),
            pl.BlockSpec((None, hd, tq), qmap),
            pl.BlockSpec((None, IDX_HEADS, tq), qmap),
            pl.BlockSpec((None, n_keys, IDX_DIM), kmap, **mode),
            pl.BlockSpec((None, n_keys, kvd), kmap, **mode),
            pl.BlockSpec((None, n_keys, kvd), kmap, **mode),
        ],
        out_specs=pl.BlockSpec((None, hd, tq), qmap),
        out_shape=jax.ShapeDtypeStruct((qt.shape[0], hd, nt * tq if prompt else tq), BF16),
        scratch_shapes=[
            pltpu.VMEM((nc, tk, tq), F32),
            pltpu.VMEM((tk, tq), F32),
            pltpu.VMEM((A_HEADS, 1, tq), F32),
            pltpu.VMEM((A_HEADS, 1, tq), F32),
            pltpu.VMEM((A_HEADS, A_HEAD_DIM, tq), F32),
        ],
        compiler_params=pltpu.CompilerParams(
            dimension_semantics=("parallel",), vmem_limit_bytes=VMEM_LIMIT),
        name="dsa_attn_prompt" if prompt else "dsa_attn_sample",
    )(qit, qt, wt, ki, k, v)


def _sample_queries_t(at):
    c = at.shape[0]
    a = at[:, SEQ:].reshape(c, DEC_BATCH, DEC_SEQ).transpose(1, 0, 2)
    return jnp.pad(a, ((0, 0), (0, 0), (0, LANES - DEC_SEQ)))


def _sample_rows(ot):
    b, c, _ = ot.shape
    return ot[..., :DEC_SEQ].transpose(0, 2, 1).reshape(b * DEC_SEQ, c)


def _swa_attn_kernel(sink_ref, qt_ref, kp_ref, kc_ref, vp_ref, vc_ref, ot_ref, *, tq, prompt, n_valid):
    j = pl.program_id(0)
    kb = jnp.concatenate([kp_ref[...], kc_ref[...]], axis=0)
    vb = jnp.concatenate([vp_ref[...], vc_ref[...]], axis=0)
    nk = kb.shape[0]
    krow = lax.broadcasted_iota(I32, (nk, tq), 0)
    if prompt:
        kchunk = krow // CHUNK
        qchunk = lax.broadcasted_iota(I32, (nk, tq), 1) // CHUNK
        wc = WINDOW // CHUNK
        mask = (kchunk >= qchunk) & (kchunk <= qchunk + wc) & ((j > 0) | (kchunk >= wc))
    else:
        mask = krow < n_valid
    c_exp = B_HEAD_DIM ** -0.5 * LOG2_E
    bias = jnp.where(mask, 0.0, NEG_MASK)
    zeros = jnp.zeros((B_HEAD_DIM, tq), BF16)
    ones_rows = jnp.ones((BF16_SUBLANES, nk), BF16)
    group = B_HEADS // B_KV_HEADS

    def logits(h):
        g = h // group
        qh = qt_ref[h * B_HEAD_DIM:(h + 1) * B_HEAD_DIM, :]
        qpad = jnp.concatenate([qh, zeros] if g % 2 == 0 else [zeros, qh], axis=0)
        kblk = kb[:, (g // 2) * LANES:(g // 2 + 1) * LANES]
        return jnp.dot(kblk, qpad, preferred_element_type=F32)

    ahead = [logits(h) for h in range(LOOKAHEAD)]
    for h in range(B_HEADS):
        if h + LOOKAHEAD < B_HEADS:
            ahead.append(logits(h + LOOKAHEAD))
        s = ahead.pop(0)
        g = h // group
        s2 = s * c_exp + bias
        sink2 = sink_ref[h] * LOG2_E
        m2 = jnp.maximum(_col_reduce(s2, jnp.max), sink2)
        pb = jnp.exp2(s2 - m2).astype(BF16)
        den = jnp.dot(ones_rows, pb, preferred_element_type=F32)[:1] + jnp.exp2(sink2 - m2)
        o2 = _tdot(vb[:, (g // 2) * LANES:(g // 2 + 1) * LANES], pb)
        oh = o2[:B_HEAD_DIM] if g % 2 == 0 else o2[B_HEAD_DIM:]
        ot_ref[h * B_HEAD_DIM:(h + 1) * B_HEAD_DIM, :] = (oh * (1.0 / den)).astype(ot_ref.dtype)


def _swa_attn(sinks, qt, k, v, *, tq, prompt, n_valid):
    hd = B_HEADS * B_HEAD_DIM
    kvd = B_KV_HEADS * B_HEAD_DIM
    if prompt:
        nt = SEQ // tq
        per = tq // WINDOW
        qmap = lambda j: (0, 0, j)
        prev = lambda j: (0, jnp.maximum(per * j - 1, 0), 0)
        cur = lambda j: (0, j, 0)
        rc = tq
    else:
        nt = qt.shape[0]
        qmap = lambda j: (j, 0, 0)
        prev = lambda j: (j, 0, 0)
        cur = lambda j: (j, 1, 0)
        rc = WINDOW
    return pl.pallas_call(
        functools.partial(_swa_attn_kernel, tq=tq, prompt=prompt, n_valid=n_valid),
        grid=(nt,),
        in_specs=[
            pl.BlockSpec(memory_space=pltpu.SMEM),
            pl.BlockSpec((None, hd, tq), qmap),
            pl.BlockSpec((None, WINDOW, kvd), prev),
            pl.BlockSpec((None, rc, kvd), cur),
            pl.BlockSpec((None, WINDOW, kvd), prev),
            pl.BlockSpec((None, rc, kvd), cur),
        ],
        out_specs=pl.BlockSpec((None, hd, tq), qmap),
        out_shape=jax.ShapeDtypeStruct((qt.shape[0], hd, nt * tq if prompt else tq), BF16),
        compiler_params=pltpu.CompilerParams(
            dimension_semantics=("parallel",), vmem_limit_bytes=VMEM_LIMIT),
        name="swa_attn_prompt" if prompt else "swa_attn_sample",
    )(sinks, qt, k, k, v, v)


def kernel(x_prompt, x_sample, cache_dsa_k, cache_dsa_v, cache_dsa_kidx, cache_swa_k, cache_swa_v,
           norm_ffn1, ffn1_gate, ffn1_up, ffn1_down, norm_mix,
           dsa_w_in, dsa_kln_g, dsa_kln_b, dsa_w_out,
           swa_w_in, swa_b_in, swa_sinks, swa_w_out,
           norm_ffn2, ffn2_gate, ffn2_up, ffn2_down, norm_final):
    n_s = DEC_BATCH * DEC_SEQ
    pos = jnp.concatenate([jnp.arange(SEQ), PAST_LEN + jnp.tile(jnp.arange(DEC_SEQ), DEC_BATCH)])

    g1, u1, d1 = ffn1_gate.astype(BF16), ffn1_up.astype(BF16), ffn1_down.astype(BF16)
    g2, u2, d2 = ffn2_gate.astype(BF16), ffn2_up.astype(BF16), ffn2_down.astype(BF16)

    x = _ffn(x_prompt.reshape(SEQ, D_MODEL), norm_ffn1[0], g1, u1, d1, 0,
             x_sample=x_sample.reshape(n_s, D_MODEL))

    a_in = sum(A_SPLITS)
    a_pad = -a_in % LANES
    w_in = jnp.pad(dsa_w_in[0], ((0, 0), (0, a_pad))).astype(BF16)
    offs = np.cumsum((0,) + A_SPLITS)
    w_scale = IDX_HEADS ** -0.5 * IDX_DIM ** -0.5
    segs = (
        (int(offs[0]), A_SPLITS[0], True, False, None, (), (0,)),
        (int(offs[1]), A_SPLITS[1], True, False, None, (1, 2), ()),
        (int(offs[2]), A_SPLITS[2], False, False, None, (3, 4), ()),
        (int(offs[3]), A_SPLITS[3], True, False, None, (), (5,)),
        (int(offs[4]), A_SPLITS[4], True, True, None, (6, 7), ()),
        (int(offs[5]), LANES, False, False, w_scale, (), (8,)),
    )
    outs = [(A_SPLITS[0], BF16, True), (A_SPLITS[1], F32, False), (A_SPLITS[1], BF16, False),
            (A_SPLITS[2], F32, False), (A_SPLITS[2], BF16, False), (A_SPLITS[3], BF16, True),
            (A_SPLITS[4], F32, False), (A_SPLITS[4], BF16, False), (LANES, F32, True)]
    qt_a, k_a, k_a16, v_a, v_a16, qit_a, ki_a, ki_a16, wt_a = _proj(
        x, norm_mix[0], w_in, _rope_tables(pos, A_ROT, A_HEAD_DIM), segs, outs, A_ROT // 2, A_HEAD_DIM,
        ln=(dsa_kln_g[0], dsa_kln_b[0]))

    ot_p = _dsa_attn(qit_a[None], qt_a[None], wt_a[None], ki_a16[None], k_a16[None], v_a16[None],
                     tq=ATT_TQ, n_keys=SEQ, prompt=True, n_valid=SEQ, topk=min(TOPK_MAX, SEQ // 4))

    l_s = PAST_LEN + DEC_SEQ
    n_keys_s = -(-l_s // ATT_TK) * ATT_TK

    def with_cache(cache, new):
        new = new[SEQ:].reshape(DEC_BATCH, DEC_SEQ, -1)
        cache = cache.reshape(DEC_BATCH, PAST_LEN, -1).astype(BF16)
        return jnp.pad(jnp.concatenate([cache, new], axis=1), ((0, 0), (0, n_keys_s - l_s), (0, 0)))

    ot_s = _dsa_attn(_sample_queries_t(qit_a), _sample_queries_t(qt_a), _sample_queries_t(wt_a[:IDX_HEADS]),
                     with_cache(cache_dsa_kidx[0], ki_a16), with_cache(cache_dsa_k[0], k_a16),
                     with_cache(cache_dsa_v[0], v_a16),
                     tq=LANES, n_keys=n_keys_s, prompt=False, n_valid=l_s, topk=min(TOPK_MAX, l_s // 4))

    x = _outproj(x, ot_p[0], _sample_rows(ot_s), dsa_w_out[0].astype(BF16))
    x = _ffn(x, norm_ffn2[0], g2, u2, d2, 0)

    x = _ffn(x, norm_ffn1[1], g1, u1, d1, 1)

    boffs = np.cumsum((0,) + B_SPLITS)
    bsegs = (
        (int(boffs[0]), B_SPLITS[0], True, False, None, (), (0,)),
        (int(boffs[1]), B_SPLITS[1], True, False, None, (1, 2), ()),
        (int(boffs[2]), B_SPLITS[2], False, False, None, (3, 4), ()),
    )
    bouts = [(B_SPLITS[0], BF16, True), (B_SPLITS[1], F32, False), (B_SPLITS[1], BF16, False),
             (B_SPLITS[2], F32, False), (B_SPLITS[2], BF16, False)]
    qt_b, k_b, k_b16, v_b, v_b16 = _proj(
        x, norm_mix[1], swa_w_in[0].astype(BF16), _rope_tables(pos, B_ROT, B_HEAD_DIM),
        bsegs, bouts, B_ROT // 2, B_HEAD_DIM, bias=swa_b_in[0])

    sinks = swa_sinks[0]
    ot_bp = _swa_attn(sinks, qt_b[None], k_b16[None], v_b16[None], tq=SWA_TQ, prompt=True, n_valid=0)

    rows = cache_swa_k.shape[2]

    def swa_keys(cache, new):
        new = new[SEQ:].reshape(DEC_BATCH, DEC_SEQ, -1)
        cache = cache.reshape(DEC_BATCH, rows, -1).astype(BF16)
        return jnp.pad(jnp.concatenate([cache, new], axis=1),
                       ((0, 0), (0, 2 * WINDOW - rows - DEC_SEQ), (0, 0)))

    ot_bs = _swa_attn(sinks, _sample_queries_t(qt_b), swa_keys(cache_swa_k[0], k_b16),
                      swa_keys(cache_swa_v[0], v_b16), tq=LANES, prompt=False, n_valid=rows + DEC_SEQ)

    x = _outproj(x, ot_bp[0], _sample_rows(ot_bs), swa_w_out[0].astype(BF16))
    y_p, y_s = _ffn(x, norm_ffn2[1], g2, u2, d2, 1, final_gain=norm_final, split_out=True)

    y_prompt = y_p.reshape(1, SEQ, D_MODEL)
    y_sample = y_s.reshape(DEC_BATCH, DEC_SEQ, D_MODEL)

    def split(a, *tail):
        return (a[:SEQ].reshape((1, 1, SEQ) + tail), a[SEQ:].reshape((1, DEC_BATCH, DEC_SEQ) + tail))

    dkp, dks = split(k_a, A_KV_HEADS, A_HEAD_DIM)
    dvp, dvs = split(v_a, A_KV_HEADS, A_HEAD_DIM)
    dip, dis = split(ki_a, IDX_DIM)
    skp, sks = split(k_b, B_KV_HEADS, B_HEAD_DIM)
    svp, svs = split(v_b, B_KV_HEADS, B_HEAD_DIM)
    wrows = min(WINDOW, SEQ)
    return (y_prompt, y_sample, dkp, dvp, dip, skp[:, :, SEQ - wrows:], svp[:, :, SEQ - wrows:],
            dks, dvs, dis, sks, svs)
```

```python
import functools
import math

import jax
import jax.numpy as jnp
import numpy as np
from jax import lax
from jax.experimental import pallas as pl
from jax.experimental.pallas import tpu as pltpu

F32 = jnp.float32
BF16 = jnp.bfloat16
I32 = jnp.int32

D_MODEL = 2048
SEQ = 8192
DEC_BATCH = 8
DEC_SEQ = 32
PAST_LEN = 1024
N_TOK = SEQ + DEC_BATCH * DEC_SEQ
CHUNK = 64
ROPE_THETA = 500000.0
NORM_EPS = 1e-6
FFN_DIM = 5632

A_HEADS = 16
A_KV_HEADS = 4
A_GROUP = A_HEADS // A_KV_HEADS
A_HEAD_DIM = 128
A_ROT = A_HEAD_DIM // 4
IDX_HEADS = 16
IDX_DIM = 128
TOPK_MAX = 256
A_SPLITS = (A_HEADS * A_HEAD_DIM, A_KV_HEADS * A_HEAD_DIM, A_KV_HEADS * A_HEAD_DIM,
            IDX_HEADS * IDX_DIM, IDX_DIM, IDX_HEADS)

B_HEADS = 32
B_KV_HEADS = 4
B_HEAD_DIM = 64
B_ROT = B_HEAD_DIM // 4
WINDOW = 128
B_SPLITS = (B_HEADS * B_HEAD_DIM, B_KV_HEADS * B_HEAD_DIM, B_KV_HEADS * B_HEAD_DIM)

LANES = 128
MXU_COLS = 256
BF16_SUBLANES = 16
VMEM_LIMIT = 56 * 1024 * 1024

FFN_TM = 704
FFN_TF = 512
PROJ_TM = 384
ATT_TQ = 256
ATT_TK = 512
ATT_TK_SAMPLE = 384
SWA_TQ = 256
LOOKAHEAD = 4
RED_ROWS = 32

INT_MIN = -2 ** 31
NEG_MASK = -2e30
NEG_INIT = -1e30
LOG2_E = math.log2(math.e)


def _rms(x, g):
    return x * lax.rsqrt(jnp.mean(x * x, axis=-1, keepdims=True) + NORM_EPS) * g


def _ffn_kernel(*refs, final_norm, split_in, split_out):
    it = iter(refs)
    x_ref = next(it)
    xs_ref = next(it) if split_in else None
    g_ref, wg_ref, wu_ref, wd_ref, gf_ref, o_ref = (next(it) for _ in range(6))
    os_ref = next(it) if split_out else None
    h_ref = next(it)
    xbuf_ref = next(it) if split_in else x_ref
    i, j = pl.program_id(0), pl.program_id(1)
    last_tile = i == pl.num_programs(0) - 1
    n_s = DEC_BATCH * DEC_SEQ
    n_p = FFN_TM - n_s

    @pl.when(j == 0)
    def _():
        if split_in:
            @pl.when(jnp.logical_not(last_tile))
            def _():
                xbuf_ref[...] = x_ref[...]

            @pl.when(last_tile)
            def _():
                xbuf_ref[:n_p] = x_ref[:n_p]
                xbuf_ref[n_p:] = xs_ref[...]
        h_ref[...] = _rms(xbuf_ref[...], g_ref[...]).astype(BF16)
        o_ref[...] = jnp.zeros_like(o_ref)

    h = h_ref[...]
    g = jnp.dot(h, wg_ref[...], preferred_element_type=F32)
    u = jnp.dot(h, wu_ref[...], preferred_element_type=F32)
    a = (g * jax.nn.sigmoid(g) * u).astype(BF16)
    o_ref[...] += jnp.dot(a, wd_ref[...], preferred_element_type=F32)

    @pl.when(j == pl.num_programs(1) - 1)
    def _():
        y = xbuf_ref[...] + 0.5 * o_ref[...]
        if final_norm:
            y = _rms(y, gf_ref[...])
        o_ref[...] = y
        if split_out:
            @pl.when(last_tile)
            def _():
                os_ref[...] = y[n_p:]


def _ffn(x, gain, wg, wu, wd, layer, final_gain=None, x_sample=None, split_out=False):
    split_in = x_sample is not None
    n_s = DEC_BATCH * DEC_SEQ
    final_norm = final_gain is not None
    gf = final_gain if final_norm else gain
    row_spec = pl.BlockSpec((FFN_TM, D_MODEL), lambda i, j: (i, 0))
    sample_spec = pl.BlockSpec((n_s, D_MODEL), lambda i, j: (0, 0))
    vec_spec = pl.BlockSpec((1, D_MODEL), lambda i, j: (0, 0))
    in_specs = [row_spec] + ([sample_spec] if split_in else []) + [
        vec_spec,
        pl.BlockSpec((None, D_MODEL, FFN_TF), lambda i, j: (layer, 0, j)),
        pl.BlockSpec((None, D_MODEL, FFN_TF), lambda i, j: (layer, 0, j)),
        pl.BlockSpec((None, FFN_TF, D_MODEL), lambda i, j: (layer, j, 0)),
        vec_spec,
    ]
    args = [x] + ([x_sample] if split_in else []) + [gain.reshape(1, D_MODEL), wg, wu, wd, gf.reshape(1, D_MODEL)]
    if split_out:
        out_specs = [row_spec, sample_spec]
        out_shape = [jax.ShapeDtypeStruct((SEQ, D_MODEL), F32), jax.ShapeDtypeStruct((n_s, D_MODEL), F32)]
    else:
        out_specs = row_spec
        out_shape = jax.ShapeDtypeStruct((N_TOK, D_MODEL), F32)
    scratch = [pltpu.VMEM((FFN_TM, D_MODEL), BF16)] + ([pltpu.VMEM((FFN_TM, D_MODEL), F32)] if split_in else [])
    return pl.pallas_call(
        functools.partial(_ffn_kernel, final_norm=final_norm, split_in=split_in, split_out=split_out),
        grid=(N_TOK // FFN_TM, FFN_DIM // FFN_TF),
        in_specs=in_specs,
        out_specs=out_specs,
        out_shape=out_shape,
        scratch_shapes=scratch,
        compiler_params=pltpu.CompilerParams(
            dimension_semantics=("arbitrary" if split_out else "parallel", "arbitrary"),
            vmem_limit_bytes=VMEM_LIMIT),
        name="ffn",
    )(*args)


def _rope_tables(pos, rot, head_dim):
    half = rot // 2
    inv_freq = jnp.float32(ROPE_THETA) ** (-jnp.arange(half, dtype=F32) / half)
    d = np.arange(LANES) % head_dim
    ang = pos.astype(F32)[:, None] * inv_freq[d % half][None, :]
    c = jnp.where(d < rot, jnp.cos(ang), 1.0)
    s = jnp.where(d < half, -jnp.sin(ang), jnp.where(d < rot, jnp.sin(ang), 0.0))
    return jnp.stack([c, s])


def _proj_kernel(*refs, segs, half, head_dim, has_bias, has_ln):
    it = iter(refs)
    x_ref, g_ref, w_ref, tab_ref = next(it), next(it), next(it), next(it)
    b_ref = next(it) if has_bias else None
    lng_ref, lnb_ref = (next(it), next(it)) if has_ln else (None, None)
    out_refs = list(it)

    h = _rms(x_ref[...], g_ref[...]).astype(BF16)
    c_tab, s_tab = tab_ref[0], tab_ref[1]
    low_half = lax.broadcasted_iota(I32, c_tab.shape, 1) % head_dim < half

    for col0, ncols, rope, ln, scale, natural, transposed in segs:
        for b0 in range(0, ncols, MXU_COLS):
            bw = min(MXU_COLS, ncols - b0)
            y = jnp.dot(h, w_ref[:, col0 + b0:col0 + b0 + bw], preferred_element_type=F32)
            if has_bias:
                y = y + b_ref[:, col0 + b0:col0 + b0 + bw]
            for s0 in range(0, bw, LANES):
                ys = y[:, s0:s0 + LANES]
                if ln:
                    mu = jnp.mean(ys, axis=-1, keepdims=True)
                    yc = ys - mu
                    var = jnp.mean(yc * yc, axis=-1, keepdims=True)
                    ys = yc * lax.rsqrt(var + NORM_EPS) * lng_ref[...] + lnb_ref[...]
                if rope:
                    rot_half = jnp.where(low_half, pltpu.roll(ys, LANES - half, 1), pltpu.roll(ys, half, 1))
                    ys = ys * c_tab + rot_half * s_tab
                if scale is not None:
                    ys = ys * scale
                c = b0 + s0
                for o in natural:
                    out_refs[o][:, c:c + LANES] = ys.astype(out_refs[o].dtype)
                if transposed:
                    yt = ys.T
                    for o in transposed:
                        out_refs[o][c:c + LANES, :] = yt.astype(out_refs[o].dtype)


def _proj(x, gain, w, tab, segs, outs, half, head_dim, bias=None, ln=None):
    t = x.shape[0]
    wcols = w.shape[1]
    in_specs = [
        pl.BlockSpec((PROJ_TM, D_MODEL), lambda i: (i, 0)),
        pl.BlockSpec((1, D_MODEL), lambda i: (0, 0)),
        pl.BlockSpec((D_MODEL, wcols), lambda i: (0, 0), pipeline_mode=pl.Buffered(1)),
        pl.BlockSpec((2, PROJ_TM, LANES), lambda i: (0, i, 0)),
    ]
    args = [x, gain.reshape(1, D_MODEL), w, tab]
    if bias is not None:
        in_specs.append(pl.BlockSpec((1, wcols), lambda i: (0, 0)))
        args.append(bias.reshape(1, wcols))
    if ln is not None:
        in_specs += [pl.BlockSpec((1, LANES), lambda i: (0, 0))] * 2
        args += [ln[0].reshape(1, LANES), ln[1].reshape(1, LANES)]
    out_specs, out_shape = [], []
    for ncols, dtype, transposed in outs:
        if transposed:
            out_specs.append(pl.BlockSpec((ncols, PROJ_TM), lambda i: (0, i)))
            out_shape.append(jax.ShapeDtypeStruct((ncols, t), dtype))
        else:
            out_specs.append(pl.BlockSpec((PROJ_TM, ncols), lambda i: (i, 0)))
            out_shape.append(jax.ShapeDtypeStruct((t, ncols), dtype))
    return pl.pallas_call(
        functools.partial(_proj_kernel, segs=segs, half=half, head_dim=head_dim,
                          has_bias=bias is not None, has_ln=ln is not None),
        grid=(t // PROJ_TM,),
        in_specs=in_specs,
        out_specs=out_specs,
        out_shape=out_shape,
        compiler_params=pltpu.CompilerParams(
            dimension_semantics=("parallel",), vmem_limit_bytes=VMEM_LIMIT),
        name="proj",
    )(*args)


def _tdot(a, b):
    return lax.dot_general(a, b, (((0,), (0,)), ((), ())), preferred_element_type=F32)


def _outproj_kernel(x_ref, ot_ref, os_ref, w_ref, y_ref, *, n_prompt_tiles):
    i = pl.program_id(0)

    @pl.when(i < n_prompt_tiles)
    def _():
        y_ref[...] = x_ref[...] + _tdot(ot_ref[...], w_ref[...])

    @pl.when(i >= n_prompt_tiles)
    def _():
        y_ref[...] = x_ref[...] + jnp.dot(os_ref[...], w_ref[...], preferred_element_type=F32)


def _outproj(x, ot_prompt, o_sample, w):
    t = x.shape[0]
    n_s = o_sample.shape[0]
    n_p = SEQ // n_s
    return pl.pallas_call(
        functools.partial(_outproj_kernel, n_prompt_tiles=n_p),
        grid=(t // n_s,),
        in_specs=[
            pl.BlockSpec((n_s, D_MODEL), lambda i: (i, 0)),
            pl.BlockSpec((ot_prompt.shape[0], n_s), lambda i: (0, jnp.minimum(i, n_p - 1))),
            pl.BlockSpec(o_sample.shape, lambda i: (0, 0)),
            pl.BlockSpec(w.shape, lambda i: (0, 0)),
        ],
        out_specs=pl.BlockSpec((n_s, D_MODEL), lambda i: (i, 0)),
        out_shape=jax.ShapeDtypeStruct((t, D_MODEL), F32),
        compiler_params=pltpu.CompilerParams(
            dimension_semantics=("parallel",), vmem_limit_bytes=VMEM_LIMIT),
        name="outproj",
    )(x, ot_prompt, o_sample, w)


def _key_to_float(key):
    bits = jnp.where(key >= 0, key, key ^ jnp.int32(0x7FFFFFFF))
    exp_mask = jnp.int32(0x7F800000)
    bits = jnp.where((bits & exp_mask) == exp_mask, bits & jnp.int32(-0x800000), bits)
    return pltpu.bitcast(bits, F32)


def _col_reduce(x, op):
    r, n = x.shape
    return op(op(x.reshape(r // RED_ROWS, RED_ROWS, n), axis=0), axis=0, keepdims=True)


def _dsa_attn_kernel(qit_ref, qt_ref, wt_ref, ki_ref, k_ref, v_ref, ot_ref,
                     s_ref, bias_ref, m_ref, l_ref, acc_ref,
                     *, tq, tk, prompt, n_chunks_static, n_valid, topk):
    j = pl.program_id(0)
    sub = tk // LANES
    qlane = lax.broadcasted_iota(I32, (1, tq), 1)
    if prompt:
        limit = ((j * tq + qlane) // CHUNK + 1) * CHUNK
        n_chunks = (j * tq + tq + tk - 1) // tk
    else:
        limit = jnp.full((1, tq), n_valid, I32)
        n_chunks = n_chunks_static
    krow = lax.broadcasted_iota(I32, (LANES, tq), 0)

    def head(ref, h):
        return ref[h * LANES:(h + 1) * LANES, :]

    def score_chunk(c, carry):
        for s in range(sub):
            k0 = pl.multiple_of(c * tk + s * LANES, LANES)
            kis = ki_ref[pl.ds(k0, LANES), :]
            acc = jnp.zeros((LANES, tq), F32)
            for h in range(IDX_HEADS):
                lg = jnp.dot(kis, head(qit_ref, h), preferred_element_type=F32)
                acc = acc + wt_ref[h:h + 1, :] * jnp.maximum(lg, 0.0)
            s_ref[c, s * LANES:(s + 1) * LANES, :] = jnp.where(k0 + krow < limit, acc, -jnp.inf)
        return carry

    lax.fori_loop(0, n_chunks, score_chunk, 0)

    ones_rows = jnp.ones((BF16_SUBLANES, tk), BF16)

    def bisect(it, prefix):
        bit = lax.shift_left(jnp.int32(1), jnp.int32(31) - it)
        cand_u = prefix | bit
        cand_f = _key_to_float(cand_u ^ jnp.int32(INT_MIN))

        return jnp.where(count(lambda s: s >= cand_f) >= float(topk), cand_u, prefix)

    def count(pred):
        def count_chunk(c, acc):
            ones = jnp.where(pred(s_ref[c]), 1.0, 0.0)
            return acc + jnp.sum(ones.reshape(tk // RED_ROWS, RED_ROWS, tq), axis=0)

        acc = lax.fori_loop(0, n_chunks, count_chunk, jnp.zeros((RED_ROWS, tq), F32))
        return jnp.sum(acc, axis=0, keepdims=True)

    prefix = lax.fori_loop(0, 32, bisect, jnp.zeros((1, tq), I32))
    tau_f = _key_to_float(jnp.maximum(prefix ^ jnp.int32(INT_MIN), jnp.int32(INT_MIN + 0x00800000)))

    ties = jnp.max(count(lambda s: s >= tau_f)) > float(topk)
    n_equal_kept = lax.cond(ties, lambda: float(topk) - count(lambda s: s > tau_f),
                            lambda: jnp.zeros((1, tq), F32))

    m_ref[...] = jnp.full(m_ref.shape, NEG_INIT, F32)
    l_ref[...] = jnp.zeros(l_ref.shape, F32)
    acc_ref[...] = jnp.zeros(acc_ref.shape, F32)
    c_exp = A_HEAD_DIM ** -0.5 * LOG2_E

    def attn_chunk(c, equal_seen):
        sc = s_ref[c]

        def select_all_equal():
            bias_ref[...] = jnp.where(sc >= tau_f, 0.0, NEG_MASK)
            return equal_seen

        def select_first_equal():
            eq = sc == tau_f
            eq01 = jnp.where(eq, 1.0, 0.0)
            row = lax.broadcasted_iota(I32, (tk, tk), 0)
            col = lax.broadcasted_iota(I32, (tk, tk), 1)
            tri = jnp.where(col <= row, 1.0, 0.0).astype(BF16)
            upto = jnp.dot(tri, eq01.astype(BF16), preferred_element_type=F32)
            keep = (sc > tau_f) | (eq & (equal_seen + upto - eq01 < n_equal_kept))
            bias_ref[...] = jnp.where(keep, 0.0, NEG_MASK)
            return equal_seen + upto[tk - 1:tk]

        equal_seen = lax.cond(ties, select_first_equal, select_all_equal)
        m_old = [m_ref[h] for h in range(A_HEADS)]
        l_old = [l_ref[h] for h in range(A_HEADS)]
        m_new, l_new = [], []

        keys = pl.ds(pl.multiple_of(c * tk, tk), tk)

        def kv_head(ref, h):
            return ref[keys, (h // A_GROUP) * A_HEAD_DIM:(h // A_GROUP + 1) * A_HEAD_DIM]

        def logits(h):
            return jnp.dot(kv_head(k_ref, h), head(qt_ref, h), preferred_element_type=F32) + bias_ref[...]

        ahead = [logits(h) for h in range(LOOKAHEAD)]
        for h in range(A_HEADS):
            if h + LOOKAHEAD < A_HEADS:
                ahead.append(logits(h + LOOKAHEAD))
            sb = ahead.pop(0)
            mn = jnp.maximum(m_old[h], _col_reduce(sb, jnp.max))
            p = jnp.exp2((sb - mn) * c_exp)
            alpha = jnp.exp2((m_old[h] - mn) * c_exp)
            m_new.append(mn)
            pb = p.astype(BF16)
            l_new.append(alpha * l_old[h] + jnp.dot(ones_rows, pb, preferred_element_type=F32)[:1])
            acc_ref[h] = alpha * acc_ref[h] + _tdot(kv_head(v_ref, h), pb)
        for h in range(A_HEADS):
            m_ref[h] = m_new[h]
            l_ref[h] = l_new[h]
        return equal_seen

    lax.fori_loop(0, n_chunks, attn_chunk, jnp.zeros((1, tq), F32))

    for h in range(A_HEADS):
        ot_ref[h * LANES:(h + 1) * LANES, :] = (acc_ref[h] / l_ref[h]).astype(ot_ref.dtype)


def _dsa_attn(qit, qt, wt, ki, k, v, *, tq, tk, n_keys, prompt, n_valid, topk):
    nc = n_keys // tk
    if prompt:
        nt = SEQ // tq
        qmap = lambda j: (0, 0, j)
        kmap = lambda j: (0, 0, 0)
        mode = dict(pipeline_mode=pl.Buffered(1))
    else:
        nt = qt.shape[0]
        qmap = lambda j: (j, 0, 0)
        kmap = lambda j: (j, 0, 0)
        mode = {}
    hd = A_HEADS * A_HEAD_DIM
    kvd = A_KV_HEADS * A_HEAD_DIM
    return pl.pallas_call(
        functools.partial(_dsa_attn_kernel, tq=tq, tk=tk, prompt=prompt, n_chunks_static=nc,
                          n_valid=n_valid, topk=topk),
        grid=(nt,),
        in_specs=[
            pl.BlockSpec((None, IDX_HEADS * IDX_DIM, tq), qmap),
            pl.BlockSpec((None, hd, tq), qmap),
            pl.BlockSpec((None, IDX_HEADS, tq), qmap),
            pl.BlockSpec((None, n_keys, IDX_DIM), kmap, **mode),
            pl.BlockSpec((None, n_keys, kvd), kmap, **mode),
            pl.BlockSpec((None, n_keys, kvd), kmap, **mode),
        ],
        out_specs=pl.BlockSpec((None, hd, tq), qmap),
        out_shape=jax.ShapeDtypeStruct((qt.shape[0], hd, nt * tq if prompt else tq), BF16),
        scratch_shapes=[
            pltpu.VMEM((nc, tk, tq), F32),
            pltpu.VMEM((tk, tq), F32),
            pltpu.VMEM((A_HEADS, 1, tq), F32),
            pltpu.VMEM((A_HEADS, 1, tq), F32),
            pltpu.VMEM((A_HEADS, A_HEAD_DIM, tq), F32),
        ],
        compiler_params=pltpu.CompilerParams(
            dimension_semantics=("parallel",), vmem_limit_bytes=VMEM_LIMIT),
        name="dsa_attn_prompt" if prompt else "dsa_attn_sample",
    )(qit, qt, wt, ki, k, v)


def _sample_queries_t(at):
    c = at.shape[0]
    a = at[:, SEQ:].reshape(c, DEC_BATCH, DEC_SEQ).transpose(1, 0, 2)
    return jnp.pad(a, ((0, 0), (0, 0), (0, LANES - DEC_SEQ)))


def _sample_rows(ot):
    b, c, _ = ot.shape
    return ot[..., :DEC_SEQ].transpose(0, 2, 1).reshape(b * DEC_SEQ, c)


def _swa_attn_kernel(sink_ref, qt_ref, kp_ref, kc_ref, vp_ref, vc_ref, ot_ref, *, tq, prompt, n_valid):
    j = pl.program_id(0)
    kb = jnp.concatenate([kp_ref[...], kc_ref[...]], axis=0)
    vb = jnp.concatenate([vp_ref[...], vc_ref[...]], axis=0)
    nk = kb.shape[0]
    krow = lax.broadcasted_iota(I32, (nk, tq), 0)
    if prompt:
        kchunk = krow // CHUNK
        qchunk = lax.broadcasted_iota(I32, (nk, tq), 1) // CHUNK
        wc = WINDOW // CHUNK
        mask = (kchunk >= qchunk) & (kchunk <= qchunk + wc) & ((j > 0) | (kchunk >= wc))
    else:
        mask = krow < n_valid
    c_exp = B_HEAD_DIM ** -0.5 * LOG2_E
    bias = jnp.where(mask, 0.0, NEG_MASK)
    zeros = jnp.zeros((B_HEAD_DIM, tq), BF16)
    ones_rows = jnp.ones((BF16_SUBLANES, nk), BF16)
    group = B_HEADS // B_KV_HEADS

    def logits(h):
        g = h // group
        qh = qt_ref[h * B_HEAD_DIM:(h + 1) * B_HEAD_DIM, :]
        qpad = jnp.concatenate([qh, zeros] if g % 2 == 0 else [zeros, qh], axis=0)
        kblk = kb[:, (g // 2) * LANES:(g // 2 + 1) * LANES]
        return jnp.dot(kblk, qpad, preferred_element_type=F32)

    ahead = [logits(h) for h in range(LOOKAHEAD)]
    for h in range(B_HEADS):
        if h + LOOKAHEAD < B_HEADS:
            ahead.append(logits(h + LOOKAHEAD))
        s = ahead.pop(0)
        g = h // group
        s2 = s * c_exp + bias
        sink2 = sink_ref[h] * LOG2_E
        m2 = jnp.maximum(_col_reduce(s2, jnp.max), sink2)
        pb = jnp.exp2(s2 - m2).astype(BF16)
        den = jnp.dot(ones_rows, pb, preferred_element_type=F32)[:1] + jnp.exp2(sink2 - m2)
        o2 = _tdot(vb[:, (g // 2) * LANES:(g // 2 + 1) * LANES], pb)
        oh = o2[:B_HEAD_DIM] if g % 2 == 0 else o2[B_HEAD_DIM:]
        ot_ref[h * B_HEAD_DIM:(h + 1) * B_HEAD_DIM, :] = (oh * (1.0 / den)).astype(ot_ref.dtype)


def _swa_attn(sinks, qt, k, v, *, tq, prompt, n_valid):
    hd = B_HEADS * B_HEAD_DIM
    kvd = B_KV_HEADS * B_HEAD_DIM
    if prompt:
        nt = SEQ // tq
        per = tq // WINDOW
        qmap = lambda j: (0, 0, j)
        prev = lambda j: (0, jnp.maximum(per * j - 1, 0), 0)
        cur = lambda j: (0, j, 0)
        rc = tq
    else:
        nt = qt.shape[0]
        qmap = lambda j: (j, 0, 0)
        prev = lambda j: (j, 0, 0)
        cur = lambda j: (j, 1, 0)
        rc = WINDOW
    return pl.pallas_call(
        functools.partial(_swa_attn_kernel, tq=tq, prompt=prompt, n_valid=n_valid),
        grid=(nt,),
        in_specs=[
            pl.BlockSpec(memory_space=pltpu.SMEM),
            pl.BlockSpec((None, hd, tq), qmap),
            pl.BlockSpec((None, WINDOW, kvd), prev),
            pl.BlockSpec((None, rc, kvd), cur),
            pl.BlockSpec((None, WINDOW, kvd), prev),
            pl.BlockSpec((None, rc, kvd), cur),
        ],
        out_specs=pl.BlockSpec((None, hd, tq), qmap),
        out_shape=jax.ShapeDtypeStruct((qt.shape[0], hd, nt * tq if prompt else tq), BF16),
        compiler_params=pltpu.CompilerParams(
            dimension_semantics=("parallel",), vmem_limit_bytes=VMEM_LIMIT),
        name="swa_attn_prompt" if prompt else "swa_attn_sample",
    )(sinks, qt, k, k, v, v)


def kernel(x_prompt, x_sample, cache_dsa_k, cache_dsa_v, cache_dsa_kidx, cache_swa_k, cache_swa_v,
           norm_ffn1, ffn1_gate, ffn1_up, ffn1_down, norm_mix,
           dsa_w_in, dsa_kln_g, dsa_kln_b, dsa_w_out,
           swa_w_in, swa_b_in, swa_sinks, swa_w_out,
           norm_ffn2, ffn2_gate, ffn2_up, ffn2_down, norm_final):
    n_s = DEC_BATCH * DEC_SEQ
    pos = jnp.concatenate([jnp.arange(SEQ), PAST_LEN + jnp.tile(jnp.arange(DEC_SEQ), DEC_BATCH)])

    g1, u1, d1 = ffn1_gate.astype(BF16), ffn1_up.astype(BF16), ffn1_down.astype(BF16)
    g2, u2, d2 = ffn2_gate.astype(BF16), ffn2_up.astype(BF16), ffn2_down.astype(BF16)

    x = _ffn(x_prompt.reshape(SEQ, D_MODEL), norm_ffn1[0], g1, u1, d1, 0,
             x_sample=x_sample.reshape(n_s, D_MODEL))

    a_in = sum(A_SPLITS)
    a_pad = -a_in % LANES
    w_in = jnp.pad(dsa_w_in[0], ((0, 0), (0, a_pad))).astype(BF16)
    offs = np.cumsum((0,) + A_SPLITS)
    w_scale = IDX_HEADS ** -0.5 * IDX_DIM ** -0.5
    segs = (
        (int(offs[0]), A_SPLITS[0], True, False, None, (), (0,)),
        (int(offs[1]), A_SPLITS[1], True, False, None, (1, 2), ()),
        (int(offs[2]), A_SPLITS[2], False, False, None, (3, 4), ()),
        (int(offs[3]), A_SPLITS[3], True, False, None, (), (5,)),
        (int(offs[4]), A_SPLITS[4], True, True, None, (6, 7), ()),
        (int(offs[5]), LANES, False, False, w_scale, (), (8,)),
    )
    outs = [(A_SPLITS[0], BF16, True), (A_SPLITS[1], F32, False), (A_SPLITS[1], BF16, False),
            (A_SPLITS[2], F32, False), (A_SPLITS[2], BF16, False), (A_SPLITS[3], BF16, True),
            (A_SPLITS[4], F32, False), (A_SPLITS[4], BF16, False), (LANES, F32, True)]
    qt_a, k_a, k_a16, v_a, v_a16, qit_a, ki_a, ki_a16, wt_a = _proj(
        x, norm_mix[0], w_in, _rope_tables(pos, A_ROT, A_HEAD_DIM), segs, outs, A_ROT // 2, A_HEAD_DIM,
        ln=(dsa_kln_g[0], dsa_kln_b[0]))

    ot_p = _dsa_attn(qit_a[None], qt_a[None], wt_a[None], ki_a16[None], k_a16[None], v_a16[None],
                     tq=ATT_TQ, tk=ATT_TK, n_keys=SEQ, prompt=True, n_valid=SEQ, topk=min(TOPK_MAX, SEQ // 4))

    l_s = PAST_LEN + DEC_SEQ
    n_keys_s = -(-l_s // ATT_TK_SAMPLE) * ATT_TK_SAMPLE

    def with_cache(cache, new):
        new = new[SEQ:].reshape(DEC_BATCH, DEC_SEQ, -1)
        cache = cache.reshape(DEC_BATCH, PAST_LEN, -1).astype(BF16)
        return jnp.pad(jnp.concatenate([cache, new], axis=1), ((0, 0), (0, n_keys_s - l_s), (0, 0)))

    ot_s = _dsa_attn(_sample_queries_t(qit_a), _sample_queries_t(qt_a), _sample_queries_t(wt_a[:IDX_HEADS]),
                     with_cache(cache_dsa_kidx[0], ki_a16), with_cache(cache_dsa_k[0], k_a16),
                     with_cache(cache_dsa_v[0], v_a16),
                     tq=LANES, tk=ATT_TK_SAMPLE, n_keys=n_keys_s, prompt=False, n_valid=l_s,
                     topk=min(TOPK_MAX, l_s // 4))

    x = _outproj(x, ot_p[0], _sample_rows(ot_s), dsa_w_out[0].astype(BF16))
    x = _ffn(x, norm_ffn2[0], g2, u2, d2, 0)

    x = _ffn(x, norm_ffn1[1], g1, u1, d1, 1)

    boffs = np.cumsum((0,) + B_SPLITS)
    bsegs = (
        (int(boffs[0]), B_SPLITS[0], True, False, None, (), (0,)),
        (int(boffs[1]), B_SPLITS[1], True, False, None, (1, 2), ()),
        (int(boffs[2]), B_SPLITS[2], False, False, None, (3, 4), ()),
    )
    bouts = [(B_SPLITS[0], BF16, True), (B_SPLITS[1], F32, False), (B_SPLITS[1], BF16, False),
             (B_SPLITS[2], F32, False), (B_SPLITS[2], BF16, False)]
    qt_b, k_b, k_b16, v_b, v_b16 = _proj(
        x, norm_mix[1], swa_w_in[0].astype(BF16), _rope_tables(pos, B_ROT, B_HEAD_DIM),
        bsegs, bouts, B_ROT // 2, B_HEAD_DIM, bias=swa_b_in[0])

    sinks = swa_sinks[0]
    ot_bp = _swa_attn(sinks, qt_b[None], k_b16[None], v_b16[None], tq=SWA_TQ, prompt=True, n_valid=0)

    rows = cache_swa_k.shape[2]

    def swa_keys(cache, new):
        new = new[SEQ:].reshape(DEC_BATCH, DEC_SEQ, -1)
        cache = cache.reshape(DEC_BATCH, rows, -1).astype(BF16)
        return jnp.pad(jnp.concatenate([cache, new], axis=1),
                       ((0, 0), (0, 2 * WINDOW - rows - DEC_SEQ), (0, 0)))

    ot_bs = _swa_attn(sinks, _sample_queries_t(qt_b), swa_keys(cache_swa_k[0], k_b16),
                      swa_keys(cache_swa_v[0], v_b16), tq=LANES, prompt=False, n_valid=rows + DEC_SEQ)

    x = _outproj(x, ot_bp[0], _sample_rows(ot_bs), swa_w_out[0].astype(BF16))
    y_p, y_s = _ffn(x, norm_ffn2[1], g2, u2, d2, 1, final_gain=norm_final, split_out=True)

    y_prompt = y_p.reshape(1, SEQ, D_MODEL)
    y_sample = y_s.reshape(DEC_BATCH, DEC_SEQ, D_MODEL)

    def split(a, *tail):
        return (a[:SEQ].reshape((1, 1, SEQ) + tail), a[SEQ:].reshape((1, DEC_BATCH, DEC_SEQ) + tail))

    dkp, dks = split(k_a, A_KV_HEADS, A_HEAD_DIM)
    dvp, dvs = split(v_a, A_KV_HEADS, A_HEAD_DIM)
    dip, dis = split(ki_a, IDX_DIM)
    skp, sks = split(k_b, B_KV_HEADS, B_HEAD_DIM)
    svp, svs = split(v_b, B_KV_HEADS, B_HEAD_DIM)
    wrows = min(WINDOW, SEQ)
    return (y_prompt, y_sample, dkp, dvp, dip, skp[:, :, SEQ - wrows:], svp[:, :, SEQ - wrows:],
            dks, dvs, dis, sks, svs)
```

```python
import functools
import math

import jax
import jax.numpy as jnp
import numpy as np
from jax import lax
from jax.experimental import pallas as pl
from jax.experimental.pallas import tpu as pltpu

F32 = jnp.float32
BF16 = jnp.bfloat16
I32 = jnp.int32

D_MODEL = 2048
SEQ = 8192
DEC_BATCH = 8
DEC_SEQ = 32
PAST_LEN = 1024
N_TOK = SEQ + DEC_BATCH * DEC_SEQ
CHUNK = 64
ROPE_THETA = 500000.0
NORM_EPS = 1e-6
FFN_DIM = 5632

A_HEADS = 16
A_KV_HEADS = 4
A_GROUP = A_HEADS // A_KV_HEADS
A_HEAD_DIM = 128
A_ROT = A_HEAD_DIM // 4
IDX_HEADS = 16
IDX_DIM = 128
TOPK_MAX = 256
A_SPLITS = (A_HEADS * A_HEAD_DIM, A_KV_HEADS * A_HEAD_DIM, A_KV_HEADS * A_HEAD_DIM,
            IDX_HEADS * IDX_DIM, IDX_DIM, IDX_HEADS)

B_HEADS = 32
B_KV_HEADS = 4
B_HEAD_DIM = 64
B_ROT = B_HEAD_DIM // 4
WINDOW = 128
B_SPLITS = (B_HEADS * B_HEAD_DIM, B_KV_HEADS * B_HEAD_DIM, B_KV_HEADS * B_HEAD_DIM)

LANES = 128
MXU_COLS = 256
BF16_SUBLANES = 16
VMEM_LIMIT = 56 * 1024 * 1024

FFN_TM = 704
FFN_TF = 512
PROJ_TM = 384
OUT_TM = 768
ATT_TQ = 256
ATT_TK = 512
ATT_TK_SAMPLE = 384
SWA_TQ = 256
LOOKAHEAD = 4
RED_ROWS = 64
COUNT_ROWS = 32

INT_MIN = -2 ** 31
NEG_MASK = -2e30
NEG_INIT = -1e30
LOG2_E = math.log2(math.e)


def _rms(x, g):
    return x * lax.rsqrt(jnp.mean(x * x, axis=-1, keepdims=True) + NORM_EPS) * g


def _ffn_kernel(*refs, final_norm, split_in, split_out):
    it = iter(refs)
    x_ref = next(it)
    xs_ref = next(it) if split_in else None
    g_ref, wg_ref, wu_ref, wd_ref, gf_ref, o_ref = (next(it) for _ in range(6))
    os_ref = next(it) if split_out else None
    h_ref = next(it)
    xbuf_ref = next(it) if split_in else x_ref
    i, j = pl.program_id(0), pl.program_id(1)
    last_tile = i == pl.num_programs(0) - 1
    n_s = DEC_BATCH * DEC_SEQ
    n_p = FFN_TM - n_s

    @pl.when(j == 0)
    def _():
        if split_in:
            @pl.when(jnp.logical_not(last_tile))
            def _():
                xbuf_ref[...] = x_ref[...]

            @pl.when(last_tile)
            def _():
                xbuf_ref[:n_p] = x_ref[:n_p]
                xbuf_ref[n_p:] = xs_ref[...]
        h_ref[...] = _rms(xbuf_ref[...], g_ref[...]).astype(BF16)
        o_ref[...] = jnp.zeros_like(o_ref)

    h = h_ref[...]
    g = jnp.dot(h, wg_ref[...], preferred_element_type=F32)
    u = jnp.dot(h, wu_ref[...], preferred_element_type=F32)
    a = (g * jax.nn.sigmoid(g) * u).astype(BF16)
    o_ref[...] += jnp.dot(a, wd_ref[...], preferred_element_type=F32)

    @pl.when(j == pl.num_programs(1) - 1)
    def _():
        y = xbuf_ref[...] + 0.5 * o_ref[...]
        if final_norm:
            y = _rms(y, gf_ref[...])
        o_ref[...] = y
        if split_out:
            @pl.when(last_tile)
            def _():
                os_ref[...] = y[n_p:]


def _ffn(x, gain, wg, wu, wd, layer, final_gain=None, x_sample=None, split_out=False):
    split_in = x_sample is not None
    n_s = DEC_BATCH * DEC_SEQ
    final_norm = final_gain is not None
    gf = final_gain if final_norm else gain
    row_spec = pl.BlockSpec((FFN_TM, D_MODEL), lambda i, j: (i, 0))
    sample_spec = pl.BlockSpec((n_s, D_MODEL), lambda i, j: (0, 0))
    vec_spec = pl.BlockSpec((1, D_MODEL), lambda i, j: (0, 0))
    in_specs = [row_spec] + ([sample_spec] if split_in else []) + [
        vec_spec,
        pl.BlockSpec((None, D_MODEL, FFN_TF), lambda i, j: (layer, 0, j)),
        pl.BlockSpec((None, D_MODEL, FFN_TF), lambda i, j: (layer, 0, j)),
        pl.BlockSpec((None, FFN_TF, D_MODEL), lambda i, j: (layer, j, 0)),
        vec_spec,
    ]
    args = [x] + ([x_sample] if split_in else []) + [gain.reshape(1, D_MODEL), wg, wu, wd, gf.reshape(1, D_MODEL)]
    if split_out:
        out_specs = [row_spec, sample_spec]
        out_shape = [jax.ShapeDtypeStruct((SEQ, D_MODEL), F32), jax.ShapeDtypeStruct((n_s, D_MODEL), F32)]
    else:
        out_specs = row_spec
        out_shape = jax.ShapeDtypeStruct((N_TOK, D_MODEL), F32)
    scratch = [pltpu.VMEM((FFN_TM, D_MODEL), BF16)] + ([pltpu.VMEM((FFN_TM, D_MODEL), F32)] if split_in else [])
    return pl.pallas_call(
        functools.partial(_ffn_kernel, final_norm=final_norm, split_in=split_in, split_out=split_out),
        grid=(N_TOK // FFN_TM, FFN_DIM // FFN_TF),
        in_specs=in_specs,
        out_specs=out_specs,
        out_shape=out_shape,
        scratch_shapes=scratch,
        compiler_params=pltpu.CompilerParams(
            dimension_semantics=("arbitrary" if split_out else "parallel", "arbitrary"),
            vmem_limit_bytes=VMEM_LIMIT),
        name="ffn",
    )(*args)


def _rope_tables(pos, rot, head_dim):
    half = rot // 2
    inv_freq = jnp.float32(ROPE_THETA) ** (-jnp.arange(half, dtype=F32) / half)
    d = np.arange(LANES) % head_dim
    ang = pos.astype(F32)[:, None] * inv_freq[d % half][None, :]
    c = jnp.where(d < rot, jnp.cos(ang), 1.0)
    s = jnp.where(d < half, -jnp.sin(ang), jnp.where(d < rot, jnp.sin(ang), 0.0))
    return jnp.stack([c, s])


def _proj_kernel(*refs, segs, half, head_dim, has_bias, has_ln):
    it = iter(refs)
    x_ref, g_ref, w_ref, tab_ref = next(it), next(it), next(it), next(it)
    b_ref = next(it) if has_bias else None
    lng_ref, lnb_ref = (next(it), next(it)) if has_ln else (None, None)
    out_refs = list(it)

    h = _rms(x_ref[...], g_ref[...]).astype(BF16)
    c_tab, s_tab = tab_ref[0], tab_ref[1]
    low_half = lax.broadcasted_iota(I32, c_tab.shape, 1) % head_dim < half

    for col0, ncols, rope, ln, scale, natural, transposed in segs:
        for b0 in range(0, ncols, MXU_COLS):
            bw = min(MXU_COLS, ncols - b0)
            y = jnp.dot(h, w_ref[:, col0 + b0:col0 + b0 + bw], preferred_element_type=F32)
            if has_bias:
                y = y + b_ref[:, col0 + b0:col0 + b0 + bw]
            for s0 in range(0, bw, LANES):
                ys = y[:, s0:s0 + LANES]
                if ln:
                    mu = jnp.mean(ys, axis=-1, keepdims=True)
                    yc = ys - mu
                    var = jnp.mean(yc * yc, axis=-1, keepdims=True)
                    ys = yc * lax.rsqrt(var + NORM_EPS) * lng_ref[...] + lnb_ref[...]
                if rope:
                    rot_half = jnp.where(low_half, pltpu.roll(ys, LANES - half, 1), pltpu.roll(ys, half, 1))
                    ys = ys * c_tab + rot_half * s_tab
                if scale is not None:
                    ys = ys * scale
                c = b0 + s0
                for o in natural:
                    out_refs[o][:, c:c + LANES] = ys.astype(out_refs[o].dtype)
                if transposed:
                    yt = ys.T
                    for o in transposed:
                        out_refs[o][c:c + LANES, :] = yt.astype(out_refs[o].dtype)


def _proj(x, gain, w, tab, segs, outs, half, head_dim, bias=None, ln=None):
    t = x.shape[0]
    wcols = w.shape[1]
    in_specs = [
        pl.BlockSpec((PROJ_TM, D_MODEL), lambda i: (i, 0)),
        pl.BlockSpec((1, D_MODEL), lambda i: (0, 0)),
        pl.BlockSpec((D_MODEL, wcols), lambda i: (0, 0), pipeline_mode=pl.Buffered(1)),
        pl.BlockSpec((2, PROJ_TM, LANES), lambda i: (0, i, 0)),
    ]
    args = [x, gain.reshape(1, D_MODEL), w, tab]
    if bias is not None:
        in_specs.append(pl.BlockSpec((1, wcols), lambda i: (0, 0)))
        args.append(bias.reshape(1, wcols))
    if ln is not None:
        in_specs += [pl.BlockSpec((1, LANES), lambda i: (0, 0))] * 2
        args += [ln[0].reshape(1, LANES), ln[1].reshape(1, LANES)]
    out_specs, out_shape = [], []
    for ncols, dtype, transposed in outs:
        if transposed:
            out_specs.append(pl.BlockSpec((ncols, PROJ_TM), lambda i: (0, i)))
            out_shape.append(jax.ShapeDtypeStruct((ncols, t), dtype))
        else:
            out_specs.append(pl.BlockSpec((PROJ_TM, ncols), lambda i: (i, 0)))
            out_shape.append(jax.ShapeDtypeStruct((t, ncols), dtype))
    return pl.pallas_call(
        functools.partial(_proj_kernel, segs=segs, half=half, head_dim=head_dim,
                          has_bias=bias is not None, has_ln=ln is not None),
        grid=(t // PROJ_TM,),
        in_specs=in_specs,
        out_specs=out_specs,
        out_shape=out_shape,
        compiler_params=pltpu.CompilerParams(
            dimension_semantics=("parallel",), vmem_limit_bytes=VMEM_LIMIT),
        name="proj",
    )(*args)


def _tdot(a, b):
    return lax.dot_general(a, b, (((0,), (0,)), ((), ())), preferred_element_type=F32)


def _outproj_kernel(x_ref, ot_ref, os_ref, w_ref, y_ref):
    n_p = OUT_TM - DEC_BATCH * DEC_SEQ
    last_tile = pl.program_id(0) == pl.num_programs(0) - 1

    @pl.when(jnp.logical_not(last_tile))
    def _():
        y_ref[...] = x_ref[...] + _tdot(ot_ref[...], w_ref[...])

    @pl.when(last_tile)
    def _():
        y_ref[:n_p] = x_ref[:n_p] + _tdot(ot_ref[:, :n_p], w_ref[...])
        y_ref[n_p:] = x_ref[n_p:] + jnp.dot(os_ref[...], w_ref[...], preferred_element_type=F32)


def _outproj(x, ot_prompt, o_sample, w):
    t = x.shape[0]
    return pl.pallas_call(
        _outproj_kernel,
        grid=(t // OUT_TM,),
        in_specs=[
            pl.BlockSpec((OUT_TM, D_MODEL), lambda i: (i, 0)),
            pl.BlockSpec((ot_prompt.shape[0], OUT_TM), lambda i: (0, i)),
            pl.BlockSpec(o_sample.shape, lambda i: (0, 0)),
            pl.BlockSpec(w.shape, lambda i: (0, 0)),
        ],
        out_specs=pl.BlockSpec((OUT_TM, D_MODEL), lambda i: (i, 0)),
        out_shape=jax.ShapeDtypeStruct((t, D_MODEL), F32),
        compiler_params=pltpu.CompilerParams(
            dimension_semantics=("parallel",), vmem_limit_bytes=VMEM_LIMIT),
        name="outproj",
    )(x, ot_prompt, o_sample, w)


def _key_to_float(key):
    bits = jnp.where(key >= 0, key, key ^ jnp.int32(0x7FFFFFFF))
    exp_mask = jnp.int32(0x7F800000)
    bits = jnp.where((bits & exp_mask) == exp_mask, bits & jnp.int32(-0x800000), bits)
    return pltpu.bitcast(bits, F32)


def _col_reduce(x, op):
    r, n = x.shape
    return op(op(x.reshape(r // RED_ROWS, RED_ROWS, n), axis=0), axis=0, keepdims=True)


def _dsa_attn_kernel(qit_ref, qt_ref, wt_ref, ki_ref, k_ref, v_ref, ot_ref,
                     s_ref, bias_ref, m_ref, l_ref, acc_ref,
                     *, tq, tk, prompt, n_chunks_static, n_valid, topk):
    j = pl.program_id(0)
    sub = tk // LANES
    qlane = lax.broadcasted_iota(I32, (1, tq), 1)
    if prompt:
        limit = ((j * tq + qlane) // CHUNK + 1) * CHUNK
        n_chunks = (j * tq + tq + tk - 1) // tk
    else:
        limit = jnp.full((1, tq), n_valid, I32)
        n_chunks = n_chunks_static
    krow = lax.broadcasted_iota(I32, (LANES, tq), 0)

    def head(ref, h):
        return ref[h * LANES:(h + 1) * LANES, :]

    def score_chunk(c, carry):
        for s in range(sub):
            k0 = pl.multiple_of(c * tk + s * LANES, LANES)
            kis = ki_ref[pl.ds(k0, LANES), :]
            acc = jnp.zeros((LANES, tq), F32)
            for h in range(IDX_HEADS):
                lg = jnp.dot(kis, head(qit_ref, h), preferred_element_type=F32)
                acc = acc + wt_ref[h:h + 1, :] * jnp.maximum(lg, 0.0)
            s_ref[c, s * LANES:(s + 1) * LANES, :] = jnp.where(k0 + krow < limit, acc, -jnp.inf)
        return carry

    lax.fori_loop(0, n_chunks, score_chunk, 0)

    ones_rows = jnp.ones((BF16_SUBLANES, tk), BF16)

    def bisect(it, carry):
        prefix, n_at_prefix = carry
        bit = lax.shift_left(jnp.int32(1), jnp.int32(31) - it)
        cand_u = prefix | bit
        cand_f = _key_to_float(cand_u ^ jnp.int32(INT_MIN))
        cnt = count(lambda s: s >= cand_f)
        accept = cnt >= float(topk)
        return jnp.where(accept, cand_u, prefix), jnp.where(accept, cnt, n_at_prefix)

    def count(pred):
        def count_chunk(c, acc):
            ones = jnp.where(pred(s_ref[c]), 1.0, 0.0)
            return acc + jnp.sum(ones.reshape(tk // COUNT_ROWS, COUNT_ROWS, tq), axis=0)

        acc = lax.fori_loop(0, n_chunks, count_chunk, jnp.zeros((COUNT_ROWS, tq), F32))
        return jnp.sum(acc, axis=0, keepdims=True)

    prefix, n_at_prefix = lax.fori_loop(0, 32, bisect, (jnp.zeros((1, tq), I32), jnp.zeros((1, tq), F32)))
    tau_key = prefix ^ jnp.int32(INT_MIN)
    lowest_finite_key = jnp.int32(INT_MIN + 0x00800000)
    tau_f = _key_to_float(jnp.maximum(tau_key, lowest_finite_key))

    ties = jnp.max(jnp.where(tau_key >= lowest_finite_key, n_at_prefix, 0.0)) > float(topk)
    n_equal_kept = lax.cond(ties, lambda: float(topk) - count(lambda s: s > tau_f),
                            lambda: jnp.zeros((1, tq), F32))

    m_ref[...] = jnp.full(m_ref.shape, NEG_INIT, F32)
    l_ref[...] = jnp.zeros(l_ref.shape, F32)
    acc_ref[...] = jnp.zeros(acc_ref.shape, F32)
    c_exp = A_HEAD_DIM ** -0.5 * LOG2_E

    def attn_chunk(c, equal_seen):
        sc = s_ref[c]

        def select_all_equal():
            bias_ref[...] = jnp.where(sc >= tau_f, 0.0, NEG_MASK)
            return equal_seen

        def select_first_equal():
            eq = sc == tau_f
            eq01 = jnp.where(eq, 1.0, 0.0)
            row = lax.broadcasted_iota(I32, (tk, tk), 0)
            col = lax.broadcasted_iota(I32, (tk, tk), 1)
            tri = jnp.where(col <= row, 1.0, 0.0).astype(BF16)
            upto = jnp.dot(tri, eq01.astype(BF16), preferred_element_type=F32)
            keep = (sc > tau_f) | (eq & (equal_seen + upto - eq01 < n_equal_kept))
            bias_ref[...] = jnp.where(keep, 0.0, NEG_MASK)
            return equal_seen + upto[tk - 1:tk]

        equal_seen = lax.cond(ties, select_first_equal, select_all_equal)
        m_old = [m_ref[h] for h in range(A_HEADS)]
        l_old = [l_ref[h] for h in range(A_HEADS)]
        m_new, l_new = [], []

        keys = pl.ds(pl.multiple_of(c * tk, tk), tk)

        def kv_head(ref, h):
            return ref[keys, (h // A_GROUP) * A_HEAD_DIM:(h // A_GROUP + 1) * A_HEAD_DIM]

        def logits(h):
            return jnp.dot(kv_head(k_ref, h), head(qt_ref, h), preferred_element_type=F32) + bias_ref[...]

        ahead = [logits(h) for h in range(LOOKAHEAD)]
        for h in range(A_HEADS):
            if h + LOOKAHEAD < A_HEADS:
                ahead.append(logits(h + LOOKAHEAD))
            sb = ahead.pop(0)
            mn = jnp.maximum(m_old[h], _col_reduce(sb, jnp.max))
            p = jnp.exp2((sb - mn) * c_exp)
            alpha = jnp.exp2((m_old[h] - mn) * c_exp)
            m_new.append(mn)
            pb = p.astype(BF16)
            l_new.append(alpha * l_old[h] + jnp.dot(ones_rows, pb, preferred_element_type=F32)[:1])
            acc_ref[h] = alpha * acc_ref[h] + _tdot(kv_head(v_ref, h), pb)
        for h in range(A_HEADS):
            m_ref[h] = m_new[h]
            l_ref[h] = l_new[h]
        return equal_seen

    lax.fori_loop(0, n_chunks, attn_chunk, jnp.zeros((1, tq), F32))

    for h in range(A_HEADS):
        ot_ref[h * LANES:(h + 1) * LANES, :] = (acc_ref[h] / l_ref[h]).astype(ot_ref.dtype)


def _dsa_attn(qit, qt, wt, ki, k, v, *, tq, tk, n_keys, prompt, n_valid, topk):
    nc = n_keys // tk
    if prompt:
        nt = SEQ // tq
        qmap = lambda j: (0, 0, j)
        kmap = lambda j: (0, 0, 0)
        mode = dict(pipeline_mode=pl.Buffered(1))
    else:
        nt = qt.shape[0]
        qmap = lambda j: (j, 0, 0)
        kmap = lambda j: (j, 0, 0)
        mode = {}
    hd = A_HEADS * A_HEAD_DIM
    kvd = A_KV_HEADS * A_HEAD_DIM
    return pl.pallas_call(
        functools.partial(_dsa_attn_kernel, tq=tq, tk=tk, prompt=prompt, n_chunks_static=nc,
                          n_valid=n_valid, topk=topk),
        grid=(nt,),
        in_specs=[
            pl.BlockSpec((None, IDX_HEADS * IDX_DIM, tq), qmap),
            pl.BlockSpec((None, hd, tq), qmap),
            pl.BlockSpec((None, IDX_HEADS, tq), qmap),
            pl.BlockSpec((None, n_keys, IDX_DIM), kmap, **mode),
            pl.BlockSpec((None, n_keys, kvd), kmap, **mode),
            pl.BlockSpec((None, n_keys, kvd), kmap, **mode),
        ],
        out_specs=pl.BlockSpec((None, hd, tq), qmap),
        out_shape=jax.ShapeDtypeStruct((qt.shape[0], hd, nt * tq if prompt else tq), BF16),
        scratch_shapes=[
            pltpu.VMEM((nc, tk, tq), F32),
            pltpu.VMEM((tk, tq), F32),
            pltpu.VMEM((A_HEADS, 1, tq), F32),
            pltpu.VMEM((A_HEADS, 1, tq), F32),
            pltpu.VMEM((A_HEADS, A_HEAD_DIM, tq), F32),
        ],
        compiler_params=pltpu.CompilerParams(
            dimension_semantics=("parallel",), vmem_limit_bytes=VMEM_LIMIT),
        name="dsa_attn_prompt" if prompt else "dsa_attn_sample",
    )(qit, qt, wt, ki, k, v)


def _sample_queries_t(at):
    c = at.shape[0]
    a = at[:, SEQ:].reshape(c, DEC_BATCH, DEC_SEQ).transpose(1, 0, 2)
    return jnp.pad(a, ((0, 0), (0, 0), (0, LANES - DEC_SEQ)))


def _sample_rows(ot):
    b, c, _ = ot.shape
    return ot[..., :DEC_SEQ].transpose(0, 2, 1).reshape(b * DEC_SEQ, c)


def _swa_attn_kernel(sink_ref, qt_ref, kp_ref, kc_ref, vp_ref, vc_ref, ot_ref, *, tq, prompt, n_valid):
    j = pl.program_id(0)
    kb = jnp.concatenate([kp_ref[...], kc_ref[...]], axis=0)
    vb = jnp.concatenate([vp_ref[...], vc_ref[...]], axis=0)
    nk = kb.shape[0]
    krow = lax.broadcasted_iota(I32, (nk, tq), 0)
    if prompt:
        kchunk = krow // CHUNK
        qchunk = lax.broadcasted_iota(I32, (nk, tq), 1) // CHUNK
        wc = WINDOW // CHUNK
        mask = (kchunk >= qchunk) & (kchunk <= qchunk + wc) & ((j > 0) | (kchunk >= wc))
    else:
        mask = krow < n_valid
    c_exp = B_HEAD_DIM ** -0.5 * LOG2_E
    bias = jnp.where(mask, 0.0, NEG_MASK)
    zeros = jnp.zeros((B_HEAD_DIM, tq), BF16)
    ones_rows = jnp.ones((BF16_SUBLANES, nk), BF16)
    group = B_HEADS // B_KV_HEADS

    def logits(h):
        g = h // group
        qh = qt_ref[h * B_HEAD_DIM:(h + 1) * B_HEAD_DIM, :]
        qpad = jnp.concatenate([qh, zeros] if g % 2 == 0 else [zeros, qh], axis=0)
        kblk = kb[:, (g // 2) * LANES:(g // 2 + 1) * LANES]
        return jnp.dot(kblk, qpad, preferred_element_type=F32)

    ahead = [logits(h) for h in range(LOOKAHEAD)]
    for h in range(B_HEADS):
        if h + LOOKAHEAD < B_HEADS:
            ahead.append(logits(h + LOOKAHEAD))
        s = ahead.pop(0)
        g = h // group
        s2 = s * c_exp + bias
        sink2 = sink_ref[h] * LOG2_E
        m2 = jnp.maximum(_col_reduce(s2, jnp.max), sink2)
        pb = jnp.exp2(s2 - m2).astype(BF16)
        den = jnp.dot(ones_rows, pb, preferred_element_type=F32)[:1] + jnp.exp2(sink2 - m2)
        o2 = _tdot(vb[:, (g // 2) * LANES:(g // 2 + 1) * LANES], pb)
        oh = o2[:B_HEAD_DIM] if g % 2 == 0 else o2[B_HEAD_DIM:]
        ot_ref[h * B_HEAD_DIM:(h + 1) * B_HEAD_DIM, :] = (oh * (1.0 / den)).astype(ot_ref.dtype)


def _swa_attn(sinks, qt, k, v, *, tq, prompt, n_valid):
    hd = B_HEADS * B_HEAD_DIM
    kvd = B_KV_HEADS * B_HEAD_DIM
    if prompt:
        nt = SEQ // tq
        per = tq // WINDOW
        qmap = lambda j: (0, 0, j)
        prev = lambda j: (0, jnp.maximum(per * j - 1, 0), 0)
        cur = lambda j: (0, j, 0)
        rc = tq
    else:
        nt = qt.shape[0]
        qmap = lambda j: (j, 0, 0)
        prev = lambda j: (j, 0, 0)
        cur = lambda j: (j, 1, 0)
        rc = WINDOW
    return pl.pallas_call(
        functools.partial(_swa_attn_kernel, tq=tq, prompt=prompt, n_valid=n_valid),
        grid=(nt,),
        in_specs=[
            pl.BlockSpec(memory_space=pltpu.SMEM),
            pl.BlockSpec((None, hd, tq), qmap),
            pl.BlockSpec((None, WINDOW, kvd), prev),
            pl.BlockSpec((None, rc, kvd), cur),
            pl.BlockSpec((None, WINDOW, kvd), prev),
            pl.BlockSpec((None, rc, kvd), cur),
        ],
        out_specs=pl.BlockSpec((None, hd, tq), qmap),
        out_shape=jax.ShapeDtypeStruct((qt.shape[0], hd, nt * tq if prompt else tq), BF16),
        compiler_params=pltpu.CompilerParams(
            dimension_semantics=("parallel",), vmem_limit_bytes=VMEM_LIMIT),
        name="swa_attn_prompt" if prompt else "swa_attn_sample",
    )(sinks, qt, k, k, v, v)


def kernel(x_prompt, x_sample, cache_dsa_k, cache_dsa_v, cache_dsa_kidx, cache_swa_k, cache_swa_v,
           norm_ffn1, ffn1_gate, ffn1_up, ffn1_down, norm_mix,
           dsa_w_in, dsa_kln_g, dsa_kln_b, dsa_w_out,
           swa_w_in, swa_b_in, swa_sinks, swa_w_out,
           norm_ffn2, ffn2_gate, ffn2_up, ffn2_down, norm_final):
    n_s = DEC_BATCH * DEC_SEQ
    pos = jnp.concatenate([jnp.arange(SEQ), PAST_LEN + jnp.tile(jnp.arange(DEC_SEQ), DEC_BATCH)])

    g1, u1, d1 = ffn1_gate.astype(BF16), ffn1_up.astype(BF16), ffn1_down.astype(BF16)
    g2, u2, d2 = ffn2_gate.astype(BF16), ffn2_up.astype(BF16), ffn2_down.astype(BF16)

    x = _ffn(x_prompt.reshape(SEQ, D_MODEL), norm_ffn1[0], g1, u1, d1, 0,
             x_sample=x_sample.reshape(n_s, D_MODEL))

    a_in = sum(A_SPLITS)
    a_pad = -a_in % LANES
    w_in = jnp.pad(dsa_w_in[0], ((0, 0), (0, a_pad))).astype(BF16)
    offs = np.cumsum((0,) + A_SPLITS)
    w_scale = IDX_HEADS ** -0.5 * IDX_DIM ** -0.5
    segs = (
        (int(offs[0]), A_SPLITS[0], True, False, None, (), (0,)),
        (int(offs[1]), A_SPLITS[1], True, False, None, (1, 2), ()),
        (int(offs[2]), A_SPLITS[2], False, False, None, (3, 4), ()),
        (int(offs[3]), A_SPLITS[3], True, False, None, (), (5,)),
        (int(offs[4]), A_SPLITS[4], True, True, None, (6, 7), ()),
        (int(offs[5]), LANES, False, False, w_scale, (), (8,)),
    )
    outs = [(A_SPLITS[0], BF16, True), (A_SPLITS[1], F32, False), (A_SPLITS[1], BF16, False),
            (A_SPLITS[2], F32, False), (A_SPLITS[2], BF16, False), (A_SPLITS[3], BF16, True),
            (A_SPLITS[4], F32, False), (A_SPLITS[4], BF16, False), (LANES, F32, True)]
    qt_a, k_a, k_a16, v_a, v_a16, qit_a, ki_a, ki_a16, wt_a = _proj(
        x, norm_mix[0], w_in, _rope_tables(pos, A_ROT, A_HEAD_DIM), segs, outs, A_ROT // 2, A_HEAD_DIM,
        ln=(dsa_kln_g[0], dsa_kln_b[0]))

    ot_p = _dsa_attn(qit_a[None], qt_a[None], wt_a[None], ki_a16[None], k_a16[None], v_a16[None],
                     tq=ATT_TQ, tk=ATT_TK, n_keys=SEQ, prompt=True, n_valid=SEQ, topk=min(TOPK_MAX, SEQ // 4))

    l_s = PAST_LEN + DEC_SEQ
    n_keys_s = -(-l_s // ATT_TK_SAMPLE) * ATT_TK_SAMPLE

    def with_cache(cache, new):
        new = new[SEQ:].reshape(DEC_BATCH, DEC_SEQ, -1)
        cache = cache.reshape(DEC_BATCH, PAST_LEN, -1).astype(BF16)
        return jnp.pad(jnp.concatenate([cache, new], axis=1), ((0, 0), (0, n_keys_s - l_s), (0, 0)))

    ot_s = _dsa_attn(_sample_queries_t(qit_a), _sample_queries_t(qt_a), _sample_queries_t(wt_a[:IDX_HEADS]),
                     with_cache(cache_dsa_kidx[0], ki_a16), with_cache(cache_dsa_k[0], k_a16),
                     with_cache(cache_dsa_v[0], v_a16),
                     tq=LANES, tk=ATT_TK_SAMPLE, n_keys=n_keys_s, prompt=False, n_valid=l_s,
                     topk=min(TOPK_MAX, l_s // 4))

    x = _outproj(x, ot_p[0], _sample_rows(ot_s), dsa_w_out[0].astype(BF16))
    x = _ffn(x, norm_ffn2[0], g2, u2, d2, 0)

    x = _ffn(x, norm_ffn1[1], g1, u1, d1, 1)

    boffs = np.cumsum((0,) + B_SPLITS)
    bsegs = (
        (int(boffs[0]), B_SPLITS[0], True, False, None, (), (0,)),
        (int(boffs[1]), B_SPLITS[1], True, False, None, (1, 2), ()),
        (int(boffs[2]), B_SPLITS[2], False, False, None, (3, 4), ()),
    )
    bouts = [(B_SPLITS[0], BF16, True), (B_SPLITS[1], F32, False), (B_SPLITS[1], BF16, False),
             (B_SPLITS[2], F32, False), (B_SPLITS[2], BF16, False)]
    qt_b, k_b, k_b16, v_b, v_b16 = _proj(
        x, norm_mix[1], swa_w_in[0].astype(BF16), _rope_tables(pos, B_ROT, B_HEAD_DIM),
        bsegs, bouts, B_ROT // 2, B_HEAD_DIM, bias=swa_b_in[0])

    sinks = swa_sinks[0]
    ot_bp = _swa_attn(sinks, qt_b[None], k_b16[None], v_b16[None], tq=SWA_TQ, prompt=True, n_valid=0)

    rows = cache_swa_k.shape[2]

    def swa_keys(cache, new):
        new = new[SEQ:].reshape(DEC_BATCH, DEC_SEQ, -1)
        cache = cache.reshape(DEC_BATCH, rows, -1).astype(BF16)
        return jnp.pad(jnp.concatenate([cache, new], axis=1),
                       ((0, 0), (0, 2 * WINDOW - rows - DEC_SEQ), (0, 0)))

    ot_bs = _swa_attn(sinks, _sample_queries_t(qt_b), swa_keys(cache_swa_k[0], k_b16),
                      swa_keys(cache_swa_v[0], v_b16), tq=LANES, prompt=False, n_valid=rows + DEC_SEQ)

    x = _outproj(x, ot_bp[0], _sample_rows(ot_bs), swa_w_out[0].astype(BF16))
    y_p, y_s = _ffn(x, norm_ffn2[1], g2, u2, d2, 1, final_gain=norm_final, split_out=True)

    y_prompt = y_p.reshape(1, SEQ, D_MODEL)
    y_sample = y_s.reshape(DEC_BATCH, DEC_SEQ, D_MODEL)

    def split(a, *tail):
        return (a[:SEQ].reshape((1, 1, SEQ) + tail), a[SEQ:].reshape((1, DEC_BATCH, DEC_SEQ) + tail))

    dkp, dks = split(k_a, A_KV_HEADS, A_HEAD_DIM)
    dvp, dvs = split(v_a, A_KV_HEADS, A_HEAD_DIM)
    dip, dis = split(ki_a, IDX_DIM)
    skp, sks = split(k_b, B_KV_HEADS, B_HEAD_DIM)
    svp, svs = split(v_b, B_KV_HEADS, B_HEAD_DIM)
    wrows = min(WINDOW, SEQ)
    return (y_prompt, y_sample, dkp, dvp, dip, skp[:, :, SEQ - wrows:], svp[:, :, SEQ - wrows:],
            dks, dvs, dis, sks, svs)
```

```python
import functools
import math

import jax
import jax.numpy as jnp
import numpy as np
from jax import lax
from jax.experimental import pallas as pl
from jax.experimental.pallas import tpu as pltpu

F32 = jnp.float32
BF16 = jnp.bfloat16
I32 = jnp.int32

D_MODEL = 2048
SEQ = 8192
DEC_BATCH = 8
DEC_SEQ = 32
PAST_LEN = 1024
N_TOK = SEQ + DEC_BATCH * DEC_SEQ
CHUNK = 64
ROPE_THETA = 500000.0
NORM_EPS = 1e-6
FFN_DIM = 5632

A_HEADS = 16
A_KV_HEADS = 4
A_GROUP = A_HEADS // A_KV_HEADS
A_HEAD_DIM = 128
A_ROT = A_HEAD_DIM // 4
IDX_HEADS = 16
IDX_DIM = 128
TOPK_MAX = 256
A_SPLITS = (A_HEADS * A_HEAD_DIM, A_KV_HEADS * A_HEAD_DIM, A_KV_HEADS * A_HEAD_DIM,
            IDX_HEADS * IDX_DIM, IDX_DIM, IDX_HEADS)

B_HEADS = 32
B_KV_HEADS = 4
B_HEAD_DIM = 64
B_ROT = B_HEAD_DIM // 4
WINDOW = 128
B_SPLITS = (B_HEADS * B_HEAD_DIM, B_KV_HEADS * B_HEAD_DIM, B_KV_HEADS * B_HEAD_DIM)

LANES = 128
MXU_COLS = 256
BF16_SUBLANES = 16
VMEM_LIMIT = 56 * 1024 * 1024

FFN_TM = 704
FFN_TF = 512
PROJ_TM = 384
OUT_TM = 768
ATT_TQ = 256
ATT_TK = 512
ATT_TK_SAMPLE = 384
SWA_TQ = 256
A_LOOKAHEAD = 6
B_LOOKAHEAD = 4
RED_ROWS = 64
COUNT_ROWS = 32

INT_MIN = -2 ** 31
NEG_MASK = -2e30
NEG_INIT = -1e30
LOG2_E = math.log2(math.e)


def _rms(x, g):
    return x * lax.rsqrt(jnp.mean(x * x, axis=-1, keepdims=True) + NORM_EPS) * g


def _ffn_kernel(*refs, final_norm, split_in, split_out):
    it = iter(refs)
    x_ref = next(it)
    xs_ref = next(it) if split_in else None
    g_ref, wg_ref, wu_ref, wd_ref, gf_ref, o_ref = (next(it) for _ in range(6))
    os_ref = next(it) if split_out else None
    h_ref = next(it)
    xbuf_ref = next(it) if split_in else x_ref
    i, j = pl.program_id(0), pl.program_id(1)
    last_tile = i == pl.num_programs(0) - 1
    n_s = DEC_BATCH * DEC_SEQ
    n_p = FFN_TM - n_s

    @pl.when(j == 0)
    def _():
        if split_in:
            @pl.when(jnp.logical_not(last_tile))
            def _():
                xbuf_ref[...] = x_ref[...]

            @pl.when(last_tile)
            def _():
                xbuf_ref[:n_p] = x_ref[:n_p]
                xbuf_ref[n_p:] = xs_ref[...]
        h_ref[...] = _rms(xbuf_ref[...], g_ref[...]).astype(BF16)
        o_ref[...] = jnp.zeros_like(o_ref)

    h = h_ref[...]
    g = jnp.dot(h, wg_ref[...], preferred_element_type=F32)
    u = jnp.dot(h, wu_ref[...], preferred_element_type=F32)
    a = (g * jax.nn.sigmoid(g) * u).astype(BF16)
    o_ref[...] += jnp.dot(a, wd_ref[...], preferred_element_type=F32)

    @pl.when(j == pl.num_programs(1) - 1)
    def _():
        y = xbuf_ref[...] + 0.5 * o_ref[...]
        if final_norm:
            y = _rms(y, gf_ref[...])
        o_ref[...] = y
        if split_out:
            @pl.when(last_tile)
            def _():
                os_ref[...] = y[n_p:]


def _ffn(x, gain, wg, wu, wd, layer, final_gain=None, x_sample=None, split_out=False):
    split_in = x_sample is not None
    n_s = DEC_BATCH * DEC_SEQ
    final_norm = final_gain is not None
    gf = final_gain if final_norm else gain
    row_spec = pl.BlockSpec((FFN_TM, D_MODEL), lambda i, j: (i, 0))
    sample_spec = pl.BlockSpec((n_s, D_MODEL), lambda i, j: (0, 0))
    vec_spec = pl.BlockSpec((1, D_MODEL), lambda i, j: (0, 0))
    in_specs = [row_spec] + ([sample_spec] if split_in else []) + [
        vec_spec,
        pl.BlockSpec((None, D_MODEL, FFN_TF), lambda i, j: (layer, 0, j)),
        pl.BlockSpec((None, D_MODEL, FFN_TF), lambda i, j: (layer, 0, j)),
        pl.BlockSpec((None, FFN_TF, D_MODEL), lambda i, j: (layer, j, 0)),
        vec_spec,
    ]
    args = [x] + ([x_sample] if split_in else []) + [gain.reshape(1, D_MODEL), wg, wu, wd, gf.reshape(1, D_MODEL)]
    if split_out:
        out_specs = [row_spec, sample_spec]
        out_shape = [jax.ShapeDtypeStruct((SEQ, D_MODEL), F32), jax.ShapeDtypeStruct((n_s, D_MODEL), F32)]
    else:
        out_specs = row_spec
        out_shape = jax.ShapeDtypeStruct((N_TOK, D_MODEL), F32)
    scratch = [pltpu.VMEM((FFN_TM, D_MODEL), BF16)] + ([pltpu.VMEM((FFN_TM, D_MODEL), F32)] if split_in else [])
    return pl.pallas_call(
        functools.partial(_ffn_kernel, final_norm=final_norm, split_in=split_in, split_out=split_out),
        grid=(N_TOK // FFN_TM, FFN_DIM // FFN_TF),
        in_specs=in_specs,
        out_specs=out_specs,
        out_shape=out_shape,
        scratch_shapes=scratch,
        compiler_params=pltpu.CompilerParams(
            dimension_semantics=("arbitrary" if split_out else "parallel", "arbitrary"),
            vmem_limit_bytes=VMEM_LIMIT),
        name="ffn",
    )(*args)


def _rope_freqs(rot, head_dim):
    half = rot // 2
    inv_freq = jnp.float32(ROPE_THETA) ** (-jnp.arange(half, dtype=F32) / half)
    return inv_freq[(np.arange(LANES) % head_dim) % half][None, :]


def _proj_kernel(*refs, segs, half, head_dim, has_bias, has_ln):
    it = iter(refs)
    x_ref, g_ref, w_ref, pos_ref, freq_ref = (next(it) for _ in range(5))
    b_ref = next(it) if has_bias else None
    lng_ref, lnb_ref = (next(it), next(it)) if has_ln else (None, None)
    out_refs = list(it)

    h = _rms(x_ref[...], g_ref[...]).astype(BF16)
    ang = pos_ref[...] * freq_ref[...]
    d = lax.broadcasted_iota(I32, ang.shape, 1) % head_dim
    low_half = d < half
    sin = jnp.sin(ang)
    c_tab = jnp.where(d < 2 * half, jnp.cos(ang), 1.0)
    s_tab = jnp.where(low_half, -sin, jnp.where(d < 2 * half, sin, 0.0))

    n_p = PROJ_TM - DEC_BATCH * DEC_SEQ

    for col0, ncols, rope, ln, scale, natural, transposed, sample in segs:
        for b0 in range(0, ncols, MXU_COLS):
            bw = min(MXU_COLS, ncols - b0)
            y = jnp.dot(h, w_ref[:, col0 + b0:col0 + b0 + bw], preferred_element_type=F32)
            if has_bias:
                y = y + b_ref[:, col0 + b0:col0 + b0 + bw]
            for s0 in range(0, bw, LANES):
                ys = y[:, s0:s0 + LANES]
                if ln:
                    mu = jnp.mean(ys, axis=-1, keepdims=True)
                    yc = ys - mu
                    var = jnp.mean(yc * yc, axis=-1, keepdims=True)
                    ys = yc * lax.rsqrt(var + NORM_EPS) * lng_ref[...] + lnb_ref[...]
                if rope:
                    rot_half = jnp.where(low_half, pltpu.roll(ys, LANES - half, 1), pltpu.roll(ys, half, 1))
                    ys = ys * c_tab + rot_half * s_tab
                if scale is not None:
                    ys = ys * scale
                c = b0 + s0
                for o in natural:
                    out_refs[o][:, c:c + LANES] = ys.astype(out_refs[o].dtype)
                for o in sample:
                    out_refs[o][:, c:c + LANES] = ys[n_p:].astype(out_refs[o].dtype)
                if transposed:
                    yt = ys.T
                    for o in transposed:
                        out_refs[o][c:c + LANES, :] = yt.astype(out_refs[o].dtype)


def _proj(x, gain, w, pos, freqs, segs, outs, half, head_dim, bias=None, ln=None):
    t = x.shape[0]
    n_s = DEC_BATCH * DEC_SEQ
    wcols = w.shape[1]
    in_specs = [
        pl.BlockSpec((PROJ_TM, D_MODEL), lambda i: (i, 0)),
        pl.BlockSpec((1, D_MODEL), lambda i: (0, 0)),
        pl.BlockSpec((D_MODEL, wcols), lambda i: (0, 0), pipeline_mode=pl.Buffered(1)),
        pl.BlockSpec((PROJ_TM, 1), lambda i: (i, 0)),
        pl.BlockSpec((1, LANES), lambda i: (0, 0)),
    ]
    args = [x, gain.reshape(1, D_MODEL), w, pos, freqs]
    if bias is not None:
        in_specs.append(pl.BlockSpec((1, wcols), lambda i: (0, 0)))
        args.append(bias.reshape(1, wcols))
    if ln is not None:
        in_specs += [pl.BlockSpec((1, LANES), lambda i: (0, 0))] * 2
        args += [ln[0].reshape(1, LANES), ln[1].reshape(1, LANES)]
    out_specs, out_shape = [], []
    for ncols, dtype, layout in outs:
        if layout == "cols":
            out_specs.append(pl.BlockSpec((ncols, PROJ_TM), lambda i: (0, i)))
            out_shape.append(jax.ShapeDtypeStruct((ncols, t), dtype))
        elif layout == "sample_rows":
            out_specs.append(pl.BlockSpec((n_s, ncols), lambda i: (0, 0)))
            out_shape.append(jax.ShapeDtypeStruct((n_s, ncols), dtype))
        else:
            out_specs.append(pl.BlockSpec((PROJ_TM, ncols), lambda i: (i, 0)))
            out_shape.append(jax.ShapeDtypeStruct((SEQ if layout == "prompt_rows" else t, ncols), dtype))
    return pl.pallas_call(
        functools.partial(_proj_kernel, segs=segs, half=half, head_dim=head_dim,
                          has_bias=bias is not None, has_ln=ln is not None),
        grid=(t // PROJ_TM,),
        in_specs=in_specs,
        out_specs=out_specs,
        out_shape=out_shape,
        compiler_params=pltpu.CompilerParams(
            dimension_semantics=("arbitrary",), vmem_limit_bytes=VMEM_LIMIT),
        name="proj",
    )(*args)


def _tdot(a, b):
    return lax.dot_general(a, b, (((0,), (0,)), ((), ())), preferred_element_type=F32)


def _outproj_kernel(x_ref, ot_ref, os_ref, w_ref, y_ref):
    n_p = OUT_TM - DEC_BATCH * DEC_SEQ
    last_tile = pl.program_id(0) == pl.num_programs(0) - 1

    @pl.when(jnp.logical_not(last_tile))
    def _():
        y_ref[...] = x_ref[...] + _tdot(ot_ref[...], w_ref[...])

    @pl.when(last_tile)
    def _():
        y_ref[:n_p] = x_ref[:n_p] + _tdot(ot_ref[:, :n_p], w_ref[...])
        y_ref[n_p:] = x_ref[n_p:] + jnp.dot(os_ref[...], w_ref[...], preferred_element_type=F32)


def _outproj(x, ot_prompt, o_sample, w):
    t = x.shape[0]
    return pl.pallas_call(
        _outproj_kernel,
        grid=(t // OUT_TM,),
        in_specs=[
            pl.BlockSpec((OUT_TM, D_MODEL), lambda i: (i, 0)),
            pl.BlockSpec((ot_prompt.shape[0], OUT_TM), lambda i: (0, i)),
            pl.BlockSpec(o_sample.shape, lambda i: (0, 0)),
            pl.BlockSpec(w.shape, lambda i: (0, 0)),
        ],
        out_specs=pl.BlockSpec((OUT_TM, D_MODEL), lambda i: (i, 0)),
        out_shape=jax.ShapeDtypeStruct((t, D_MODEL), F32),
        compiler_params=pltpu.CompilerParams(
            dimension_semantics=("parallel",), vmem_limit_bytes=VMEM_LIMIT),
        name="outproj",
    )(x, ot_prompt, o_sample, w)


def _key_to_float(key):
    bits = jnp.where(key >= 0, key, key ^ jnp.int32(0x7FFFFFFF))
    exp_mask = jnp.int32(0x7F800000)
    bits = jnp.where((bits & exp_mask) == exp_mask, bits & jnp.int32(-0x800000), bits)
    return pltpu.bitcast(bits, F32)


def _col_reduce(x, op):
    r, n = x.shape
    return op(op(x.reshape(r // RED_ROWS, RED_ROWS, n), axis=0), axis=0, keepdims=True)


def _dsa_attn_kernel(qit_ref, qt_ref, wt_ref, ki_ref, k_ref, v_ref, ot_ref,
                     s_ref, bias_ref, m_ref, l_ref, acc_ref,
                     *, tq, tk, prompt, n_chunks_static, n_valid, topk):
    j = pl.program_id(0)
    sub = tk // LANES
    qlane = lax.broadcasted_iota(I32, (1, tq), 1)
    if prompt:
        limit = ((j * tq + qlane) // CHUNK + 1) * CHUNK
        n_chunks = (j * tq + tq + tk - 1) // tk
    else:
        limit = jnp.full((1, tq), n_valid, I32)
        n_chunks = n_chunks_static
    krow = lax.broadcasted_iota(I32, (LANES, tq), 0)

    def head(ref, h):
        return ref[h * LANES:(h + 1) * LANES, :]

    def score_chunk(c, carry):
        for s in range(sub):
            k0 = pl.multiple_of(c * tk + s * LANES, LANES)
            kis = ki_ref[pl.ds(k0, LANES), :]
            acc = jnp.zeros((LANES, tq), F32)
            for h in range(IDX_HEADS):
                lg = jnp.dot(kis, head(qit_ref, h), preferred_element_type=F32)
                acc = acc + wt_ref[h:h + 1, :] * jnp.maximum(lg, 0.0)
            s_ref[c, s * LANES:(s + 1) * LANES, :] = jnp.where(k0 + krow < limit, acc, -jnp.inf)
        return carry

    lax.fori_loop(0, n_chunks, score_chunk, 0)

    ones_rows = jnp.ones((BF16_SUBLANES, tk), BF16)

    def bisect(it, carry):
        prefix, n_at_prefix = carry
        bit = lax.shift_left(jnp.int32(1), jnp.int32(31) - it)
        cand_u = prefix | bit
        cand_f = _key_to_float(cand_u ^ jnp.int32(INT_MIN))
        cnt = count(lambda s: s >= cand_f)
        accept = cnt >= float(topk)
        return jnp.where(accept, cand_u, prefix), jnp.where(accept, cnt, n_at_prefix)

    def count(pred):
        def count_chunk(c, acc):
            ones = jnp.where(pred(s_ref[c]), 1.0, 0.0)
            return acc + jnp.sum(ones.reshape(tk // COUNT_ROWS, COUNT_ROWS, tq), axis=0)

        acc = lax.fori_loop(0, n_chunks, count_chunk, jnp.zeros((COUNT_ROWS, tq), F32))
        return jnp.sum(acc, axis=0, keepdims=True)

    prefix, n_at_prefix = lax.fori_loop(0, 32, bisect, (jnp.zeros((1, tq), I32), jnp.zeros((1, tq), F32)))
    tau_key = prefix ^ jnp.int32(INT_MIN)
    lowest_finite_key = jnp.int32(INT_MIN + 0x00800000)
    tau_f = _key_to_float(jnp.maximum(tau_key, lowest_finite_key))

    ties = jnp.max(jnp.where(tau_key >= lowest_finite_key, n_at_prefix, 0.0)) > float(topk)
    n_equal_kept = lax.cond(ties, lambda: float(topk) - count(lambda s: s > tau_f),
                            lambda: jnp.zeros((1, tq), F32))

    m_ref[...] = jnp.full(m_ref.shape, NEG_INIT, F32)
    l_ref[...] = jnp.zeros(l_ref.shape, F32)
    acc_ref[...] = jnp.zeros(acc_ref.shape, F32)
    c_exp = A_HEAD_DIM ** -0.5 * LOG2_E

    def attn_chunk(c, equal_seen):
        sc = s_ref[c]

        def select_all_equal():
            bias_ref[...] = jnp.where(sc >= tau_f, 0.0, NEG_MASK)
            return equal_seen

        def select_first_equal():
            eq = sc == tau_f
            eq01 = jnp.where(eq, 1.0, 0.0)
            row = lax.broadcasted_iota(I32, (tk, tk), 0)
            col = lax.broadcasted_iota(I32, (tk, tk), 1)
            tri = jnp.where(col <= row, 1.0, 0.0).astype(BF16)
            upto = jnp.dot(tri, eq01.astype(BF16), preferred_element_type=F32)
            keep = (sc > tau_f) | (eq & (equal_seen + upto - eq01 < n_equal_kept))
            bias_ref[...] = jnp.where(keep, 0.0, NEG_MASK)
            return equal_seen + upto[tk - 1:tk]

        equal_seen = lax.cond(ties, select_first_equal, select_all_equal)
        m_old = [m_ref[h] for h in range(A_HEADS)]
        l_old = [l_ref[h] for h in range(A_HEADS)]
        m_new, l_new = [], []

        keys = pl.ds(pl.multiple_of(c * tk, tk), tk)

        def kv_head(ref, h):
            return ref[keys, (h // A_GROUP) * A_HEAD_DIM:(h // A_GROUP + 1) * A_HEAD_DIM]

        def logits(h):
            return jnp.dot(kv_head(k_ref, h), head(qt_ref, h), preferred_element_type=F32) + bias_ref[...]

        ahead = [logits(h) for h in range(A_LOOKAHEAD)]
        for h in range(A_HEADS):
            if h + A_LOOKAHEAD < A_HEADS:
                ahead.append(logits(h + A_LOOKAHEAD))
            sb = ahead.pop(0)
            mn = jnp.maximum(m_old[h], _col_reduce(sb, jnp.max))
            p = jnp.exp2((sb - mn) * c_exp)
            alpha = jnp.exp2((m_old[h] - mn) * c_exp)
            m_new.append(mn)
            pb = p.astype(BF16)
            l_new.append(alpha * l_old[h] + jnp.dot(ones_rows, pb, preferred_element_type=F32)[:1])
            acc_ref[h] = alpha * acc_ref[h] + _tdot(kv_head(v_ref, h), pb)
        for h in range(A_HEADS):
            m_ref[h] = m_new[h]
            l_ref[h] = l_new[h]
        return equal_seen

    lax.fori_loop(0, n_chunks, attn_chunk, jnp.zeros((1, tq), F32))

    for h in range(A_HEADS):
        ot_ref[h * LANES:(h + 1) * LANES, :] = (acc_ref[h] / l_ref[h]).astype(ot_ref.dtype)


def _dsa_attn(qit, qt, wt, ki, k, v, *, tq, tk, n_keys, prompt, n_valid, topk):
    nc = n_keys // tk
    if prompt:
        nt = SEQ // tq
        qmap = lambda j: (0, 0, j)
        kmap = lambda j: (0, 0, 0)
        mode = dict(pipeline_mode=pl.Buffered(1))
    else:
        nt = qt.shape[0]
        qmap = lambda j: (j, 0, 0)
        kmap = lambda j: (j, 0, 0)
        mode = {}
    hd = A_HEADS * A_HEAD_DIM
    kvd = A_KV_HEADS * A_HEAD_DIM
    return pl.pallas_call(
        functools.partial(_dsa_attn_kernel, tq=tq, tk=tk, prompt=prompt, n_chunks_static=nc,
                          n_valid=n_valid, topk=topk),
        grid=(nt,),
        in_specs=[
            pl.BlockSpec((None, IDX_HEADS * IDX_DIM, tq), qmap),
            pl.BlockSpec((None, hd, tq), qmap),
            pl.BlockSpec((None, IDX_HEADS, tq), qmap),
            pl.BlockSpec((None, n_keys, IDX_DIM), kmap, **mode),
            pl.BlockSpec((None, n_keys, kvd), kmap, **mode),
            pl.BlockSpec((None, n_keys, kvd), kmap, **mode),
        ],
        out_specs=pl.BlockSpec((None, hd, tq), qmap),
        out_shape=jax.ShapeDtypeStruct((qt.shape[0], hd, nt * tq if prompt else tq), BF16),
        scratch_shapes=[
            pltpu.VMEM((nc, tk, tq), F32),
            pltpu.VMEM((tk, tq), F32),
            pltpu.VMEM((A_HEADS, 1, tq), F32),
            pltpu.VMEM((A_HEADS, 1, tq), F32),
            pltpu.VMEM((A_HEADS, A_HEAD_DIM, tq), F32),
        ],
        compiler_params=pltpu.CompilerParams(
            dimension_semantics=("parallel",), vmem_limit_bytes=VMEM_LIMIT),
        name="dsa_attn_prompt" if prompt else "dsa_attn_sample",
    )(qit, qt, wt, ki, k, v)


def _sample_queries_t(at):
    c = at.shape[0]
    a = at[:, SEQ:].reshape(c, DEC_BATCH, DEC_SEQ).transpose(1, 0, 2)
    return jnp.pad(a, ((0, 0), (0, 0), (0, LANES - DEC_SEQ)))


def _sample_rows(ot):
    b, c, _ = ot.shape
    return ot[..., :DEC_SEQ].transpose(0, 2, 1).reshape(b * DEC_SEQ, c)


def _swa_attn_kernel(sink_ref, qt_ref, kp_ref, kc_ref, vp_ref, vc_ref, ot_ref, *, tq, prompt, n_valid):
    j = pl.program_id(0)
    kb = jnp.concatenate([kp_ref[...], kc_ref[...]], axis=0)
    vb = jnp.concatenate([vp_ref[...], vc_ref[...]], axis=0)
    nk = kb.shape[0]
    krow = lax.broadcasted_iota(I32, (nk, tq), 0)
    if prompt:
        kchunk = krow // CHUNK
        qchunk = lax.broadcasted_iota(I32, (nk, tq), 1) // CHUNK
        wc = WINDOW // CHUNK
        mask = (kchunk >= qchunk) & (kchunk <= qchunk + wc) & ((j > 0) | (kchunk >= wc))
    else:
        mask = krow < n_valid
    c_exp = B_HEAD_DIM ** -0.5 * LOG2_E
    bias = jnp.where(mask, 0.0, NEG_MASK)
    zeros = jnp.zeros((B_HEAD_DIM, tq), BF16)
    ones_rows = jnp.ones((BF16_SUBLANES, nk), BF16)
    group = B_HEADS // B_KV_HEADS

    def logits(h):
        g = h // group
        qh = qt_ref[h * B_HEAD_DIM:(h + 1) * B_HEAD_DIM, :]
        qpad = jnp.concatenate([qh, zeros] if g % 2 == 0 else [zeros, qh], axis=0)
        kblk = kb[:, (g // 2) * LANES:(g // 2 + 1) * LANES]
        return jnp.dot(kblk, qpad, preferred_element_type=F32)

    ahead = [logits(h) for h in range(B_LOOKAHEAD)]
    for h in range(B_HEADS):
        if h + B_LOOKAHEAD < B_HEADS:
            ahead.append(logits(h + B_LOOKAHEAD))
        s = ahead.pop(0)
        g = h // group
        s2 = s * c_exp + bias
        sink2 = sink_ref[h] * LOG2_E
        m2 = jnp.maximum(_col_reduce(s2, jnp.max), sink2)
        pb = jnp.exp2(s2 - m2).astype(BF16)
        den = jnp.dot(ones_rows, pb, preferred_element_type=F32)[:1] + jnp.exp2(sink2 - m2)
        o2 = _tdot(vb[:, (g // 2) * LANES:(g // 2 + 1) * LANES], pb)
        oh = o2[:B_HEAD_DIM] if g % 2 == 0 else o2[B_HEAD_DIM:]
        ot_ref[h * B_HEAD_DIM:(h + 1) * B_HEAD_DIM, :] = (oh * (1.0 / den)).astype(ot_ref.dtype)


def _swa_attn(sinks, qt, k, v, *, tq, prompt, n_valid):
    hd = B_HEADS * B_HEAD_DIM
    kvd = B_KV_HEADS * B_HEAD_DIM
    if prompt:
        nt = SEQ // tq
        per = tq // WINDOW
        qmap = lambda j: (0, 0, j)
        prev = lambda j: (0, jnp.maximum(per * j - 1, 0), 0)
        cur = lambda j: (0, j, 0)
        rc = tq
    else:
        nt = qt.shape[0]
        qmap = lambda j: (j, 0, 0)
        prev = lambda j: (j, 0, 0)
        cur = lambda j: (j, 1, 0)
        rc = WINDOW
    return pl.pallas_call(
        functools.partial(_swa_attn_kernel, tq=tq, prompt=prompt, n_valid=n_valid),
        grid=(nt,),
        in_specs=[
            pl.BlockSpec(memory_space=pltpu.SMEM),
            pl.BlockSpec((None, hd, tq), qmap),
            pl.BlockSpec((None, WINDOW, kvd), prev),
            pl.BlockSpec((None, rc, kvd), cur),
            pl.BlockSpec((None, WINDOW, kvd), prev),
            pl.BlockSpec((None, rc, kvd), cur),
        ],
        out_specs=pl.BlockSpec((None, hd, tq), qmap),
        out_shape=jax.ShapeDtypeStruct((qt.shape[0], hd, nt * tq if prompt else tq), BF16),
        compiler_params=pltpu.CompilerParams(
            dimension_semantics=("parallel",), vmem_limit_bytes=VMEM_LIMIT),
        name="swa_attn_prompt" if prompt else "swa_attn_sample",
    )(sinks, qt, k, k, v, v)


def kernel(x_prompt, x_sample, cache_dsa_k, cache_dsa_v, cache_dsa_kidx, cache_swa_k, cache_swa_v,
           norm_ffn1, ffn1_gate, ffn1_up, ffn1_down, norm_mix,
           dsa_w_in, dsa_kln_g, dsa_kln_b, dsa_w_out,
           swa_w_in, swa_b_in, swa_sinks, swa_w_out,
           norm_ffn2, ffn2_gate, ffn2_up, ffn2_down, norm_final):
    n_s = DEC_BATCH * DEC_SEQ
    pos = jnp.concatenate([jnp.arange(SEQ), PAST_LEN + jnp.tile(jnp.arange(DEC_SEQ), DEC_BATCH)])
    pos = pos.astype(F32)[:, None]

    g1, u1, d1 = ffn1_gate.astype(BF16), ffn1_up.astype(BF16), ffn1_down.astype(BF16)
    g2, u2, d2 = ffn2_gate.astype(BF16), ffn2_up.astype(BF16), ffn2_down.astype(BF16)

    x = _ffn(x_prompt.reshape(SEQ, D_MODEL), norm_ffn1[0], g1, u1, d1, 0,
             x_sample=x_sample.reshape(n_s, D_MODEL))

    a_in = sum(A_SPLITS)
    a_pad = -a_in % LANES
    w_in = jnp.pad(dsa_w_in[0], ((0, 0), (0, a_pad))).astype(BF16)
    offs = np.cumsum((0,) + A_SPLITS)
    w_scale = IDX_HEADS ** -0.5 * IDX_DIM ** -0.5
    segs = (
        (int(offs[0]), A_SPLITS[0], True, False, None, (), (0,), ()),
        (int(offs[1]), A_SPLITS[1], True, False, None, (1, 3), (), (2,)),
        (int(offs[2]), A_SPLITS[2], False, False, None, (4, 6), (), (5,)),
        (int(offs[3]), A_SPLITS[3], True, False, None, (), (7,), ()),
        (int(offs[4]), A_SPLITS[4], True, True, None, (8, 10), (), (9,)),
        (int(offs[5]), LANES, False, False, w_scale, (), (11,), ()),
    )
    outs = [(A_SPLITS[0], BF16, "cols"),
            (A_SPLITS[1], F32, "prompt_rows"), (A_SPLITS[1], F32, "sample_rows"), (A_SPLITS[1], BF16, "rows"),
            (A_SPLITS[2], F32, "prompt_rows"), (A_SPLITS[2], F32, "sample_rows"), (A_SPLITS[2], BF16, "rows"),
            (A_SPLITS[3], BF16, "cols"),
            (A_SPLITS[4], F32, "prompt_rows"), (A_SPLITS[4], F32, "sample_rows"), (A_SPLITS[4], BF16, "rows"),
            (LANES, F32, "cols")]
    qt_a, k_ap, k_as, k_a16, v_ap, v_as, v_a16, qit_a, ki_ap, ki_as, ki_a16, wt_a = _proj(
        x, norm_mix[0], w_in, pos, _rope_freqs(A_ROT, A_HEAD_DIM), segs, outs, A_ROT // 2, A_HEAD_DIM,
        ln=(dsa_kln_g[0], dsa_kln_b[0]))

    ot_p = _dsa_attn(qit_a[None], qt_a[None], wt_a[None], ki_a16[None], k_a16[None], v_a16[None],
                     tq=ATT_TQ, tk=ATT_TK, n_keys=SEQ, prompt=True, n_valid=SEQ, topk=min(TOPK_MAX, SEQ // 4))

    l_s = PAST_LEN + DEC_SEQ
    n_keys_s = -(-l_s // ATT_TK_SAMPLE) * ATT_TK_SAMPLE

    def with_cache(cache, new):
        new = new[SEQ:].reshape(DEC_BATCH, DEC_SEQ, -1)
        cache = cache.reshape(DEC_BATCH, PAST_LEN, -1).astype(BF16)
        return jnp.pad(jnp.concatenate([cache, new], axis=1), ((0, 0), (0, n_keys_s - l_s), (0, 0)))

    ot_s = _dsa_attn(_sample_queries_t(qit_a), _sample_queries_t(qt_a), _sample_queries_t(wt_a[:IDX_HEADS]),
                     with_cache(cache_dsa_kidx[0], ki_a16), with_cache(cache_dsa_k[0], k_a16),
                     with_cache(cache_dsa_v[0], v_a16),
                     tq=LANES, tk=ATT_TK_SAMPLE, n_keys=n_keys_s, prompt=False, n_valid=l_s,
                     topk=min(TOPK_MAX, l_s // 4))

    x = _outproj(x, ot_p[0], _sample_rows(ot_s), dsa_w_out[0].astype(BF16))
    x = _ffn(x, norm_ffn2[0], g2, u2, d2, 0)

    x = _ffn(x, norm_ffn1[1], g1, u1, d1, 1)

    boffs = np.cumsum((0,) + B_SPLITS)
    bsegs = (
        (int(boffs[0]), B_SPLITS[0], True, False, None, (), (0,), ()),
        (int(boffs[1]), B_SPLITS[1], True, False, None, (1, 3), (), (2,)),
        (int(boffs[2]), B_SPLITS[2], False, False, None, (4, 6), (), (5,)),
    )
    bouts = [(B_SPLITS[0], BF16, "cols"),
             (B_SPLITS[1], F32, "prompt_rows"), (B_SPLITS[1], F32, "sample_rows"), (B_SPLITS[1], BF16, "rows"),
             (B_SPLITS[2], F32, "prompt_rows"), (B_SPLITS[2], F32, "sample_rows"), (B_SPLITS[2], BF16, "rows")]
    qt_b, k_bp, k_bs, k_b16, v_bp, v_bs, v_b16 = _proj(
        x, norm_mix[1], swa_w_in[0].astype(BF16), pos, _rope_freqs(B_ROT, B_HEAD_DIM),
        bsegs, bouts, B_ROT // 2, B_HEAD_DIM, bias=swa_b_in[0])

    sinks = swa_sinks[0]
    ot_bp = _swa_attn(sinks, qt_b[None], k_b16[None], v_b16[None], tq=SWA_TQ, prompt=True, n_valid=0)

    rows = cache_swa_k.shape[2]

    def swa_keys(cache, new):
        new = new[SEQ:].reshape(DEC_BATCH, DEC_SEQ, -1)
        cache = cache.reshape(DEC_BATCH, rows, -1).astype(BF16)
        return jnp.pad(jnp.concatenate([cache, new], axis=1),
                       ((0, 0), (0, 2 * WINDOW - rows - DEC_SEQ), (0, 0)))

    ot_bs = _swa_attn(sinks, _sample_queries_t(qt_b), swa_keys(cache_swa_k[0], k_b16),
                      swa_keys(cache_swa_v[0], v_b16), tq=LANES, prompt=False, n_valid=rows + DEC_SEQ)

    x = _outproj(x, ot_bp[0], _sample_rows(ot_bs), swa_w_out[0].astype(BF16))
    y_p, y_s = _ffn(x, norm_ffn2[1], g2, u2, d2, 1, final_gain=norm_final, split_out=True)

    y_prompt = y_p.reshape(1, SEQ, D_MODEL)
    y_sample = y_s.reshape(DEC_BATCH, DEC_SEQ, D_MODEL)

    def shaped(prompt, sample, *tail):
        return prompt.reshape((1, 1, SEQ) + tail), sample.reshape((1, DEC_BATCH, DEC_SEQ) + tail)

    dkp, dks = shaped(k_ap, k_as, A_KV_HEADS, A_HEAD_DIM)
    dvp, dvs = shaped(v_ap, v_as, A_KV_HEADS, A_HEAD_DIM)
    dip, dis = shaped(ki_ap, ki_as, IDX_DIM)
    skp, sks = shaped(k_bp, k_bs, B_KV_HEADS, B_HEAD_DIM)
    svp, svs = shaped(v_bp, v_bs, B_KV_HEADS, B_HEAD_DIM)
    wrows = min(WINDOW, SEQ)
    return (y_prompt, y_sample, dkp, dvp, dip, skp[:, :, SEQ - wrows:], svp[:, :, SEQ - wrows:],
            dks, dvs, dis, sks, svs)
```

```python
import functools
import math

import jax
import jax.numpy as jnp
import numpy as np
from jax import lax
from jax.experimental import pallas as pl
from jax.experimental.pallas import tpu as pltpu

F32 = jnp.float32
BF16 = jnp.bfloat16
I32 = jnp.int32

D_MODEL = 2048
SEQ = 8192
DEC_BATCH = 8
DEC_SEQ = 32
PAST_LEN = 1024
N_TOK = SEQ + DEC_BATCH * DEC_SEQ
CHUNK = 64
ROPE_THETA = 500000.0
NORM_EPS = 1e-6
FFN_DIM = 5632

A_HEADS = 16
A_KV_HEADS = 4
A_GROUP = A_HEADS // A_KV_HEADS
A_HEAD_DIM = 128
A_ROT = A_HEAD_DIM // 4
IDX_HEADS = 16
IDX_DIM = 128
TOPK_MAX = 256
A_SPLITS = (A_HEADS * A_HEAD_DIM, A_KV_HEADS * A_HEAD_DIM, A_KV_HEADS * A_HEAD_DIM,
            IDX_HEADS * IDX_DIM, IDX_DIM, IDX_HEADS)

B_HEADS = 32
B_KV_HEADS = 4
B_HEAD_DIM = 64
B_ROT = B_HEAD_DIM // 4
WINDOW = 128
B_SPLITS = (B_HEADS * B_HEAD_DIM, B_KV_HEADS * B_HEAD_DIM, B_KV_HEADS * B_HEAD_DIM)

LANES = 128
MXU_COLS = 256
BF16_SUBLANES = 16
VMEM_LIMIT = 56 * 1024 * 1024

FFN_TM = 704
FFN_TF = 512
PROJ_TM = 384
OUT_TM = 768
ATT_TQ = 256
ATT_TK = 512
ATT_TK_SAMPLE = 384
SWA_TQ = 256
A_LOOKAHEAD = 6
B_LOOKAHEAD = 4
RED_ROWS = 64
COUNT_ROWS = 32

INT_MIN = -2 ** 31
NEG_MASK = -2e30
NEG_INIT = -1e30
LOG2_E = math.log2(math.e)


def _rms(x, g):
    return x * lax.rsqrt(jnp.mean(x * x, axis=-1, keepdims=True) + NORM_EPS) * g


def _ffn_kernel(*refs, final_norm, split_in, split_out):
    it = iter(refs)
    x_ref = next(it)
    xs_ref = next(it) if split_in else None
    g_ref, wg_ref, wu_ref, wd_ref, gf_ref, o_ref = (next(it) for _ in range(6))
    os_ref = next(it) if split_out else None
    h_ref = next(it)
    xbuf_ref = next(it) if split_in else x_ref
    i, j = pl.program_id(0), pl.program_id(1)
    last_tile = i == pl.num_programs(0) - 1
    n_s = DEC_BATCH * DEC_SEQ
    n_p = FFN_TM - n_s

    @pl.when(j == 0)
    def _():
        if split_in:
            @pl.when(jnp.logical_not(last_tile))
            def _():
                xbuf_ref[...] = x_ref[...]

            @pl.when(last_tile)
            def _():
                xbuf_ref[:n_p] = x_ref[:n_p]
                xbuf_ref[n_p:] = xs_ref[...]
        h_ref[...] = _rms(xbuf_ref[...], g_ref[...]).astype(BF16)
        o_ref[...] = jnp.zeros_like(o_ref)

    h = h_ref[...]
    g = jnp.dot(h, wg_ref[...], preferred_element_type=F32)
    u = jnp.dot(h, wu_ref[...], preferred_element_type=F32)
    a = (g * jax.nn.sigmoid(g) * u).astype(BF16)
    o_ref[...] += jnp.dot(a, wd_ref[...], preferred_element_type=F32)

    @pl.when(j == pl.num_programs(1) - 1)
    def _():
        y = xbuf_ref[...] + 0.5 * o_ref[...]
        if final_norm:
            y = _rms(y, gf_ref[...])
        o_ref[...] = y
        if split_out:
            @pl.when(last_tile)
            def _():
                os_ref[...] = y[n_p:]


def _ffn(x, gain, wg, wu, wd, layer, final_gain=None, x_sample=None, split_out=False):
    split_in = x_sample is not None
    n_s = DEC_BATCH * DEC_SEQ
    final_norm = final_gain is not None
    gf = final_gain if final_norm else gain
    row_spec = pl.BlockSpec((FFN_TM, D_MODEL), lambda i, j: (i, 0))
    sample_spec = pl.BlockSpec((n_s, D_MODEL), lambda i, j: (0, 0))
    vec_spec = pl.BlockSpec((1, D_MODEL), lambda i, j: (0, 0))
    in_specs = [row_spec] + ([sample_spec] if split_in else []) + [
        vec_spec,
        pl.BlockSpec((None, D_MODEL, FFN_TF), lambda i, j: (layer, 0, j)),
        pl.BlockSpec((None, D_MODEL, FFN_TF), lambda i, j: (layer, 0, j)),
        pl.BlockSpec((None, FFN_TF, D_MODEL), lambda i, j: (layer, j, 0)),
        vec_spec,
    ]
    args = [x] + ([x_sample] if split_in else []) + [gain.reshape(1, D_MODEL), wg, wu, wd, gf.reshape(1, D_MODEL)]
    if split_out:
        out_specs = [row_spec, sample_spec]
        out_shape = [jax.ShapeDtypeStruct((SEQ, D_MODEL), F32), jax.ShapeDtypeStruct((n_s, D_MODEL), F32)]
    else:
        out_specs = row_spec
        out_shape = jax.ShapeDtypeStruct((N_TOK, D_MODEL), F32)
    scratch = [pltpu.VMEM((FFN_TM, D_MODEL), BF16)] + ([pltpu.VMEM((FFN_TM, D_MODEL), F32)] if split_in else [])
    return pl.pallas_call(
        functools.partial(_ffn_kernel, final_norm=final_norm, split_in=split_in, split_out=split_out),
        grid=(N_TOK // FFN_TM, FFN_DIM // FFN_TF),
        in_specs=in_specs,
        out_specs=out_specs,
        out_shape=out_shape,
        scratch_shapes=scratch,
        compiler_params=pltpu.CompilerParams(
            dimension_semantics=("arbitrary" if split_out else "parallel", "arbitrary"),
            vmem_limit_bytes=VMEM_LIMIT),
        name="ffn",
    )(*args)


def _rope_freqs(rot, head_dim):
    half = rot // 2
    inv_freq = jnp.float32(ROPE_THETA) ** (-jnp.arange(half, dtype=F32) / half)
    return inv_freq[(np.arange(LANES) % head_dim) % half][None, :]


def _proj_kernel(*refs, segs, half, head_dim, has_bias, has_ln):
    it = iter(refs)
    x_ref, g_ref, w_ref, pos_ref, freq_ref = (next(it) for _ in range(5))
    b_ref = next(it) if has_bias else None
    lng_ref, lnb_ref = (next(it), next(it)) if has_ln else (None, None)
    out_refs = list(it)

    h = _rms(x_ref[...], g_ref[...]).astype(BF16)
    ang = pos_ref[...] * freq_ref[...]
    d = lax.broadcasted_iota(I32, ang.shape, 1) % head_dim
    low_half = d < half
    sin = jnp.sin(ang)
    c_tab = jnp.where(d < 2 * half, jnp.cos(ang), 1.0)
    s_tab = jnp.where(low_half, -sin, jnp.where(d < 2 * half, sin, 0.0))

    n_p = PROJ_TM - DEC_BATCH * DEC_SEQ

    for col0, ncols, rope, ln, scale, natural, transposed, sample in segs:
        for b0 in range(0, ncols, MXU_COLS):
            bw = min(MXU_COLS, ncols - b0)
            y = jnp.dot(h, w_ref[:, col0 + b0:col0 + b0 + bw], preferred_element_type=F32)
            if has_bias:
                y = y + b_ref[:, col0 + b0:col0 + b0 + bw]
            for s0 in range(0, bw, LANES):
                ys = y[:, s0:s0 + LANES]
                if ln:
                    mu = jnp.mean(ys, axis=-1, keepdims=True)
                    yc = ys - mu
                    var = jnp.mean(yc * yc, axis=-1, keepdims=True)
                    ys = yc * lax.rsqrt(var + NORM_EPS) * lng_ref[...] + lnb_ref[...]
                if rope:
                    rot_half = jnp.where(low_half, pltpu.roll(ys, LANES - half, 1), pltpu.roll(ys, half, 1))
                    ys = ys * c_tab + rot_half * s_tab
                if scale is not None:
                    ys = ys * scale
                c = b0 + s0
                for o in natural:
                    out_refs[o][:, c:c + LANES] = ys.astype(out_refs[o].dtype)
                for o in sample:
                    out_refs[o][:, c:c + LANES] = ys[n_p:].astype(out_refs[o].dtype)
                if transposed:
                    yt = ys.T
                    for o in transposed:
                        out_refs[o][c:c + LANES, :] = yt.astype(out_refs[o].dtype)


def _proj(x, gain, w, pos, freqs, segs, outs, half, head_dim, bias=None, ln=None):
    t = x.shape[0]
    n_s = DEC_BATCH * DEC_SEQ
    wcols = w.shape[1]
    in_specs = [
        pl.BlockSpec((PROJ_TM, D_MODEL), lambda i: (i, 0)),
        pl.BlockSpec((1, D_MODEL), lambda i: (0, 0)),
        pl.BlockSpec((D_MODEL, wcols), lambda i: (0, 0), pipeline_mode=pl.Buffered(1)),
        pl.BlockSpec((PROJ_TM, 1), lambda i: (i, 0)),
        pl.BlockSpec((1, LANES), lambda i: (0, 0)),
    ]
    args = [x, gain.reshape(1, D_MODEL), w, pos, freqs]
    if bias is not None:
        in_specs.append(pl.BlockSpec((1, wcols), lambda i: (0, 0)))
        args.append(bias.reshape(1, wcols))
    if ln is not None:
        in_specs += [pl.BlockSpec((1, LANES), lambda i: (0, 0))] * 2
        args += [ln[0].reshape(1, LANES), ln[1].reshape(1, LANES)]
    out_specs, out_shape = [], []
    for ncols, dtype, layout in outs:
        if layout == "cols":
            out_specs.append(pl.BlockSpec((ncols, PROJ_TM), lambda i: (0, i)))
            out_shape.append(jax.ShapeDtypeStruct((ncols, t), dtype))
        elif layout == "sample_rows":
            out_specs.append(pl.BlockSpec((n_s, ncols), lambda i: (0, 0)))
            out_shape.append(jax.ShapeDtypeStruct((n_s, ncols), dtype))
        else:
            out_specs.append(pl.BlockSpec((PROJ_TM, ncols), lambda i: (i, 0)))
            out_shape.append(jax.ShapeDtypeStruct((SEQ if layout == "prompt_rows" else t, ncols), dtype))
    return pl.pallas_call(
        functools.partial(_proj_kernel, segs=segs, half=half, head_dim=head_dim,
                          has_bias=bias is not None, has_ln=ln is not None),
        grid=(t // PROJ_TM,),
        in_specs=in_specs,
        out_specs=out_specs,
        out_shape=out_shape,
        compiler_params=pltpu.CompilerParams(
            dimension_semantics=("arbitrary",), vmem_limit_bytes=VMEM_LIMIT),
        name="proj",
    )(*args)


def _tdot(a, b):
    return lax.dot_general(a, b, (((0,), (0,)), ((), ())), preferred_element_type=F32)


def _outproj_kernel(x_ref, ot_ref, os_ref, w_ref, y_ref):
    n_p = OUT_TM - DEC_BATCH * DEC_SEQ
    last_tile = pl.program_id(0) == pl.num_programs(0) - 1

    @pl.when(jnp.logical_not(last_tile))
    def _():
        y_ref[...] = x_ref[...] + _tdot(ot_ref[...], w_ref[...])

    @pl.when(last_tile)
    def _():
        y_ref[:n_p] = x_ref[:n_p] + _tdot(ot_ref[:, :n_p], w_ref[...])
        y_ref[n_p:] = x_ref[n_p:] + jnp.dot(os_ref[...], w_ref[...], preferred_element_type=F32)


def _outproj(x, ot_prompt, o_sample, w):
    t = x.shape[0]
    return pl.pallas_call(
        _outproj_kernel,
        grid=(t // OUT_TM,),
        in_specs=[
            pl.BlockSpec((OUT_TM, D_MODEL), lambda i: (i, 0)),
            pl.BlockSpec((ot_prompt.shape[0], OUT_TM), lambda i: (0, i)),
            pl.BlockSpec(o_sample.shape, lambda i: (0, 0)),
            pl.BlockSpec(w.shape, lambda i: (0, 0)),
        ],
        out_specs=pl.BlockSpec((OUT_TM, D_MODEL), lambda i: (i, 0)),
        out_shape=jax.ShapeDtypeStruct((t, D_MODEL), F32),
        compiler_params=pltpu.CompilerParams(
            dimension_semantics=("parallel",), vmem_limit_bytes=VMEM_LIMIT),
        name="outproj",
    )(x, ot_prompt, o_sample, w)


def _key_to_float(key):
    bits = jnp.where(key >= 0, key, key ^ jnp.int32(0x7FFFFFFF))
    exp_mask = jnp.int32(0x7F800000)
    bits = jnp.where((bits & exp_mask) == exp_mask, bits & jnp.int32(-0x800000), bits)
    return pltpu.bitcast(bits, F32)


def _col_reduce(x, op):
    r, n = x.shape
    return op(op(x.reshape(r // RED_ROWS, RED_ROWS, n), axis=0), axis=0, keepdims=True)


def _dsa_attn_kernel(qit_ref, qt_ref, wt_ref, ki_ref, k_ref, v_ref, ot_ref,
                     s_ref, bias_ref, m_ref, l_ref, acc_ref,
                     *, tq, tk, prompt, n_chunks_static, n_valid, topk):
    j = pl.program_id(0)
    sub = tk // LANES
    qlane = lax.broadcasted_iota(I32, (1, tq), 1)
    if prompt:
        limit = ((j * tq + qlane) // CHUNK + 1) * CHUNK
        n_chunks = (j * tq + tq + tk - 1) // tk
    else:
        limit = jnp.full((1, tq), n_valid, I32)
        n_chunks = n_chunks_static
    krow = lax.broadcasted_iota(I32, (LANES, tq), 0)

    def head(ref, h):
        return ref[h * LANES:(h + 1) * LANES, :]

    def score_chunk(c, carry):
        for s in range(sub):
            k0 = pl.multiple_of(c * tk + s * LANES, LANES)
            kis = ki_ref[pl.ds(k0, LANES), :]
            acc = jnp.zeros((LANES, tq), F32)
            for h in range(IDX_HEADS):
                lg = jnp.dot(kis, head(qit_ref, h), preferred_element_type=F32)
                acc = acc + wt_ref[h:h + 1, :] * jnp.maximum(lg, 0.0)
            s_ref[c, s * LANES:(s + 1) * LANES, :] = jnp.where(k0 + krow < limit, acc, -jnp.inf)
        return carry

    lax.fori_loop(0, n_chunks, score_chunk, 0)

    ones_rows = jnp.ones((BF16_SUBLANES, tk), BF16)

    def bisect(it, carry):
        prefix, n_at_prefix = carry
        bit = lax.shift_left(jnp.int32(1), jnp.int32(31) - it)
        cand_u = prefix | bit
        cand_f = _key_to_float(cand_u ^ jnp.int32(INT_MIN))
        cnt = count(lambda s: s >= cand_f)
        accept = cnt >= float(topk)
        return jnp.where(accept, cand_u, prefix), jnp.where(accept, cnt, n_at_prefix)

    def count(pred):
        def count_chunk(c, acc):
            ones = jnp.where(pred(s_ref[c]), 1.0, 0.0)
            return acc + jnp.sum(ones.reshape(tk // COUNT_ROWS, COUNT_ROWS, tq), axis=0)

        acc = lax.fori_loop(0, n_chunks, count_chunk, jnp.zeros((COUNT_ROWS, tq), F32))
        return jnp.sum(acc, axis=0, keepdims=True)

    prefix, n_at_prefix = lax.fori_loop(0, 32, bisect, (jnp.zeros((1, tq), I32), jnp.zeros((1, tq), F32)))
    tau_key = prefix ^ jnp.int32(INT_MIN)
    lowest_finite_key = jnp.int32(INT_MIN + 0x00800000)
    tau_f = _key_to_float(jnp.maximum(tau_key, lowest_finite_key))

    ties = jnp.max(jnp.where(tau_key >= lowest_finite_key, n_at_prefix, 0.0)) > float(topk)
    n_equal_kept = lax.cond(ties, lambda: float(topk) - count(lambda s: s > tau_f),
                            lambda: jnp.zeros((1, tq), F32))

    m_ref[...] = jnp.full(m_ref.shape, NEG_INIT, F32)
    l_ref[...] = jnp.zeros(l_ref.shape, F32)
    acc_ref[...] = jnp.zeros(acc_ref.shape, F32)
    c_exp = A_HEAD_DIM ** -0.5 * LOG2_E

    def attn_chunk(c, equal_seen, *, rank_equal):
        sc = s_ref[c]
        if rank_equal:
            eq = sc == tau_f
            eq01 = jnp.where(eq, 1.0, 0.0)
            row = lax.broadcasted_iota(I32, (tk, tk), 0)
            col = lax.broadcasted_iota(I32, (tk, tk), 1)
            tri = jnp.where(col <= row, 1.0, 0.0).astype(BF16)
            upto = jnp.dot(tri, eq01.astype(BF16), preferred_element_type=F32)
            keep = (sc > tau_f) | (eq & (equal_seen + upto - eq01 < n_equal_kept))
            bias_ref[...] = jnp.where(keep, 0.0, NEG_MASK)
            equal_seen = equal_seen + upto[tk - 1:tk]
        else:
            bias_ref[...] = jnp.where(sc >= tau_f, 0.0, NEG_MASK)
        m_old = [m_ref[h] for h in range(A_HEADS)]
        l_old = [l_ref[h] for h in range(A_HEADS)]
        m_new, l_new = [], []

        keys = pl.ds(pl.multiple_of(c * tk, tk), tk)

        def kv_head(ref, h):
            return ref[keys, (h // A_GROUP) * A_HEAD_DIM:(h // A_GROUP + 1) * A_HEAD_DIM]

        def logits(h):
            return jnp.dot(kv_head(k_ref, h), head(qt_ref, h), preferred_element_type=F32) + bias_ref[...]

        ahead = [logits(h) for h in range(A_LOOKAHEAD)]
        for h in range(A_HEADS):
            if h + A_LOOKAHEAD < A_HEADS:
                ahead.append(logits(h + A_LOOKAHEAD))
            sb = ahead.pop(0)
            mn = jnp.maximum(m_old[h], _col_reduce(sb, jnp.max))
            p = jnp.exp2((sb - mn) * c_exp)
            alpha = jnp.exp2((m_old[h] - mn) * c_exp)
            m_new.append(mn)
            pb = p.astype(BF16)
            l_new.append(alpha * l_old[h] + jnp.dot(ones_rows, pb, preferred_element_type=F32)[:1])
            acc_ref[h] = alpha * acc_ref[h] + _tdot(kv_head(v_ref, h), pb)
        for h in range(A_HEADS):
            m_ref[h] = m_new[h]
            l_ref[h] = l_new[h]
        return equal_seen

    def attend(rank_equal):
        return lax.fori_loop(0, n_chunks, functools.partial(attn_chunk, rank_equal=rank_equal),
                             jnp.zeros((1, tq), F32))

    lax.cond(ties, lambda: attend(True), lambda: attend(False))

    for h in range(A_HEADS):
        ot_ref[h * LANES:(h + 1) * LANES, :] = (acc_ref[h] / l_ref[h]).astype(ot_ref.dtype)


def _dsa_attn(qit, qt, wt, ki, k, v, *, tq, tk, n_keys, prompt, n_valid, topk):
    nc = n_keys // tk
    if prompt:
        nt = SEQ // tq
        qmap = lambda j: (0, 0, j)
        kmap = lambda j: (0, 0, 0)
        mode = dict(pipeline_mode=pl.Buffered(1))
    else:
        nt = qt.shape[0]
        qmap = lambda j: (j, 0, 0)
        kmap = lambda j: (j, 0, 0)
        mode = {}
    hd = A_HEADS * A_HEAD_DIM
    kvd = A_KV_HEADS * A_HEAD_DIM
    return pl.pallas_call(
        functools.partial(_dsa_attn_kernel, tq=tq, tk=tk, prompt=prompt, n_chunks_static=nc,
                          n_valid=n_valid, topk=topk),
        grid=(nt,),
        in_specs=[
            pl.BlockSpec((None, IDX_HEADS * IDX_DIM, tq), qmap),
            pl.BlockSpec((None, hd, tq), qmap),
            pl.BlockSpec((None, IDX_HEADS, tq), qmap),
            pl.BlockSpec((None, n_keys, IDX_DIM), kmap, **mode),
            pl.BlockSpec((None, n_keys, kvd), kmap, **mode),
            pl.BlockSpec((None, n_keys, kvd), kmap, **mode),
        ],
        out_specs=pl.BlockSpec((None, hd, tq), qmap),
        out_shape=jax.ShapeDtypeStruct((qt.shape[0], hd, nt * tq if prompt else tq), BF16),
        scratch_shapes=[
            pltpu.VMEM((nc, tk, tq), F32),
            pltpu.VMEM((tk, tq), F32),
            pltpu.VMEM((A_HEADS, 1, tq), F32),
            pltpu.VMEM((A_HEADS, 1, tq), F32),
            pltpu.VMEM((A_HEADS, A_HEAD_DIM, tq), F32),
        ],
        compiler_params=pltpu.CompilerParams(
            dimension_semantics=("parallel",), vmem_limit_bytes=VMEM_LIMIT),
        name="dsa_attn_prompt" if prompt else "dsa_attn_sample",
    )(qit, qt, wt, ki, k, v)


def _sample_queries_t(at):
    c = at.shape[0]
    a = at[:, SEQ:].reshape(c, DEC_BATCH, DEC_SEQ).transpose(1, 0, 2)
    return jnp.pad(a, ((0, 0), (0, 0), (0, LANES - DEC_SEQ)))


def _sample_rows(ot):
    b, c, _ = ot.shape
    return ot[..., :DEC_SEQ].transpose(0, 2, 1).reshape(b * DEC_SEQ, c)


def _swa_attn_kernel(sink_ref, qt_ref, kp_ref, kc_ref, vp_ref, vc_ref, ot_ref, *, tq, prompt, n_valid):
    j = pl.program_id(0)
    kb = jnp.concatenate([kp_ref[...], kc_ref[...]], axis=0)
    vb = jnp.concatenate([vp_ref[...], vc_ref[...]], axis=0)
    nk = kb.shape[0]
    krow = lax.broadcasted_iota(I32, (nk, tq), 0)
    if prompt:
        kchunk = krow // CHUNK
        qchunk = lax.broadcasted_iota(I32, (nk, tq), 1) // CHUNK
        wc = WINDOW // CHUNK
        mask = (kchunk >= qchunk) & (kchunk <= qchunk + wc) & ((j > 0) | (kchunk >= wc))
    else:
        mask = krow < n_valid
    c_exp = B_HEAD_DIM ** -0.5 * LOG2_E
    bias = jnp.where(mask, 0.0, NEG_MASK)
    zeros = jnp.zeros((B_HEAD_DIM, tq), BF16)
    ones_rows = jnp.ones((BF16_SUBLANES, nk), BF16)
    group = B_HEADS // B_KV_HEADS

    def logits(h):
        g = h // group
        qh = qt_ref[h * B_HEAD_DIM:(h + 1) * B_HEAD_DIM, :]
        qpad = jnp.concatenate([qh, zeros] if g % 2 == 0 else [zeros, qh], axis=0)
        kblk = kb[:, (g // 2) * LANES:(g // 2 + 1) * LANES]
        return jnp.dot(kblk, qpad, preferred_element_type=F32)

    ahead = [logits(h) for h in range(B_LOOKAHEAD)]
    for h in range(B_HEADS):
        if h + B_LOOKAHEAD < B_HEADS:
            ahead.append(logits(h + B_LOOKAHEAD))
        s = ahead.pop(0)
        g = h // group
        s2 = s * c_exp + bias
        sink2 = sink_ref[h] * LOG2_E
        m2 = jnp.maximum(_col_reduce(s2, jnp.max), sink2)
        pb = jnp.exp2(s2 - m2).astype(BF16)
        den = jnp.dot(ones_rows, pb, preferred_element_type=F32)[:1] + jnp.exp2(sink2 - m2)
        o2 = _tdot(vb[:, (g // 2) * LANES:(g // 2 + 1) * LANES], pb)
        oh = o2[:B_HEAD_DIM] if g % 2 == 0 else o2[B_HEAD_DIM:]
        ot_ref[h * B_HEAD_DIM:(h + 1) * B_HEAD_DIM, :] = (oh * (1.0 / den)).astype(ot_ref.dtype)


def _swa_attn(sinks, qt, k, v, *, tq, prompt, n_valid):
    hd = B_HEADS * B_HEAD_DIM
    kvd = B_KV_HEADS * B_HEAD_DIM
    if prompt:
        nt = SEQ // tq
        per = tq // WINDOW
        qmap = lambda j: (0, 0, j)
        prev = lambda j: (0, jnp.maximum(per * j - 1, 0), 0)
        cur = lambda j: (0, j, 0)
        rc = tq
    else:
        nt = qt.shape[0]
        qmap = lambda j: (j, 0, 0)
        prev = lambda j: (j, 0, 0)
        cur = lambda j: (j, 1, 0)
        rc = WINDOW
    return pl.pallas_call(
        functools.partial(_swa_attn_kernel, tq=tq, prompt=prompt, n_valid=n_valid),
        grid=(nt,),
        in_specs=[
            pl.BlockSpec(memory_space=pltpu.SMEM),
            pl.BlockSpec((None, hd, tq), qmap),
            pl.BlockSpec((None, WINDOW, kvd), prev),
            pl.BlockSpec((None, rc, kvd), cur),
            pl.BlockSpec((None, WINDOW, kvd), prev),
            pl.BlockSpec((None, rc, kvd), cur),
        ],
        out_specs=pl.BlockSpec((None, hd, tq), qmap),
        out_shape=jax.ShapeDtypeStruct((qt.shape[0], hd, nt * tq if prompt else tq), BF16),
        compiler_params=pltpu.CompilerParams(
            dimension_semantics=("parallel",), vmem_limit_bytes=VMEM_LIMIT),
        name="swa_attn_prompt" if prompt else "swa_attn_sample",
    )(sinks, qt, k, k, v, v)


def kernel(x_prompt, x_sample, cache_dsa_k, cache_dsa_v, cache_dsa_kidx, cache_swa_k, cache_swa_v,
           norm_ffn1, ffn1_gate, ffn1_up, ffn1_down, norm_mix,
           dsa_w_in, dsa_kln_g, dsa_kln_b, dsa_w_out,
           swa_w_in, swa_b_in, swa_sinks, swa_w_out,
           norm_ffn2, ffn2_gate, ffn2_up, ffn2_down, norm_final):
    n_s = DEC_BATCH * DEC_SEQ
    pos = jnp.concatenate([jnp.arange(SEQ), PAST_LEN + jnp.tile(jnp.arange(DEC_SEQ), DEC_BATCH)])
    pos = pos.astype(F32)[:, None]

    g1, u1, d1 = ffn1_gate.astype(BF16), ffn1_up.astype(BF16), ffn1_down.astype(BF16)
    g2, u2, d2 = ffn2_gate.astype(BF16), ffn2_up.astype(BF16), ffn2_down.astype(BF16)

    x = _ffn(x_prompt.reshape(SEQ, D_MODEL), norm_ffn1[0], g1, u1, d1, 0,
             x_sample=x_sample.reshape(n_s, D_MODEL))

    a_in = sum(A_SPLITS)
    a_pad = -a_in % LANES
    w_in = jnp.pad(dsa_w_in[0], ((0, 0), (0, a_pad))).astype(BF16)
    offs = np.cumsum((0,) + A_SPLITS)
    w_scale = IDX_HEADS ** -0.5 * IDX_DIM ** -0.5
    segs = (
        (int(offs[0]), A_SPLITS[0], True, False, None, (), (0,), ()),
        (int(offs[1]), A_SPLITS[1], True, False, None, (1, 3), (), (2,)),
        (int(offs[2]), A_SPLITS[2], False, False, None, (4, 6), (), (5,)),
        (int(offs[3]), A_SPLITS[3], True, False, None, (), (7,), ()),
        (int(offs[4]), A_SPLITS[4], True, True, None, (8, 10), (), (9,)),
        (int(offs[5]), LANES, False, False, w_scale, (), (11,), ()),
    )
    outs = [(A_SPLITS[0], BF16, "cols"),
            (A_SPLITS[1], F32, "prompt_rows"), (A_SPLITS[1], F32, "sample_rows"), (A_SPLITS[1], BF16, "rows"),
            (A_SPLITS[2], F32, "prompt_rows"), (A_SPLITS[2], F32, "sample_rows"), (A_SPLITS[2], BF16, "rows"),
            (A_SPLITS[3], BF16, "cols"),
            (A_SPLITS[4], F32, "prompt_rows"), (A_SPLITS[4], F32, "sample_rows"), (A_SPLITS[4], BF16, "rows"),
            (LANES, F32, "cols")]
    qt_a, k_ap, k_as, k_a16, v_ap, v_as, v_a16, qit_a, ki_ap, ki_as, ki_a16, wt_a = _proj(
        x, norm_mix[0], w_in, pos, _rope_freqs(A_ROT, A_HEAD_DIM), segs, outs, A_ROT // 2, A_HEAD_DIM,
        ln=(dsa_kln_g[0], dsa_kln_b[0]))

    ot_p = _dsa_attn(qit_a[None], qt_a[None], wt_a[None], ki_a16[None], k_a16[None], v_a16[None],
                     tq=ATT_TQ, tk=ATT_TK, n_keys=SEQ, prompt=True, n_valid=SEQ, topk=min(TOPK_MAX, SEQ // 4))

    l_s = PAST_LEN + DEC_SEQ
    n_keys_s = -(-l_s // ATT_TK_SAMPLE) * ATT_TK_SAMPLE

    def with_cache(cache, new):
        new = new[SEQ:].reshape(DEC_BATCH, DEC_SEQ, -1)
        cache = cache.reshape(DEC_BATCH, PAST_LEN, -1).astype(BF16)
        return jnp.pad(jnp.concatenate([cache, new], axis=1), ((0, 0), (0, n_keys_s - l_s), (0, 0)))

    ot_s = _dsa_attn(_sample_queries_t(qit_a), _sample_queries_t(qt_a), _sample_queries_t(wt_a[:IDX_HEADS]),
                     with_cache(cache_dsa_kidx[0], ki_a16), with_cache(cache_dsa_k[0], k_a16),
                     with_cache(cache_dsa_v[0], v_a16),
                     tq=LANES, tk=ATT_TK_SAMPLE, n_keys=n_keys_s, prompt=False, n_valid=l_s,
                     topk=min(TOPK_MAX, l_s // 4))

    x = _outproj(x, ot_p[0], _sample_rows(ot_s), dsa_w_out[0].astype(BF16))
    x = _ffn(x, norm_ffn2[0], g2, u2, d2, 0)

    x = _ffn(x, norm_ffn1[1], g1, u1, d1, 1)

    boffs = np.cumsum((0,) + B_SPLITS)
    bsegs = (
        (int(boffs[0]), B_SPLITS[0], True, False, None, (), (0,), ()),
        (int(boffs[1]), B_SPLITS[1], True, False, None, (1, 3), (), (2,)),
        (int(boffs[2]), B_SPLITS[2], False, False, None, (4, 6), (), (5,)),
    )
    bouts = [(B_SPLITS[0], BF16, "cols"),
             (B_SPLITS[1], F32, "prompt_rows"), (B_SPLITS[1], F32, "sample_rows"), (B_SPLITS[1], BF16, "rows"),
             (B_SPLITS[2], F32, "prompt_rows"), (B_SPLITS[2], F32, "sample_rows"), (B_SPLITS[2], BF16, "rows")]
    qt_b, k_bp, k_bs, k_b16, v_bp, v_bs, v_b16 = _proj(
        x, norm_mix[1], swa_w_in[0].astype(BF16), pos, _rope_freqs(B_ROT, B_HEAD_DIM),
        bsegs, bouts, B_ROT // 2, B_HEAD_DIM, bias=swa_b_in[0])

    sinks = swa_sinks[0]
    ot_bp = _swa_attn(sinks, qt_b[None], k_b16[None], v_b16[None], tq=SWA_TQ, prompt=True, n_valid=0)

    rows = cache_swa_k.shape[2]

    def swa_keys(cache, new):
        new = new[SEQ:].reshape(DEC_BATCH, DEC_SEQ, -1)
        cache = cache.reshape(DEC_BATCH, rows, -1).astype(BF16)
        return jnp.pad(jnp.concatenate([cache, new], axis=1),
                       ((0, 0), (0, 2 * WINDOW - rows - DEC_SEQ), (0, 0)))

    ot_bs = _swa_attn(sinks, _sample_queries_t(qt_b), swa_keys(cache_swa_k[0], k_b16),
                      swa_keys(cache_swa_v[0], v_b16), tq=LANES, prompt=False, n_valid=rows + DEC_SEQ)

    x = _outproj(x, ot_bp[0], _sample_rows(ot_bs), swa_w_out[0].astype(BF16))
    y_p, y_s = _ffn(x, norm_ffn2[1], g2, u2, d2, 1, final_gain=norm_final, split_out=True)

    y_prompt = y_p.reshape(1, SEQ, D_MODEL)
    y_sample = y_s.reshape(DEC_BATCH, DEC_SEQ, D_MODEL)

    def shaped(prompt, sample, *tail):
        return prompt.reshape((1, 1, SEQ) + tail), sample.reshape((1, DEC_BATCH, DEC_SEQ) + tail)

    dkp, dks = shaped(k_ap, k_as, A_KV_HEADS, A_HEAD_DIM)
    dvp, dvs = shaped(v_ap, v_as, A_KV_HEADS, A_HEAD_DIM)
    dip, dis = shaped(ki_ap, ki_as, IDX_DIM)
    skp, sks = shaped(k_bp, k_bs, B_KV_HEADS, B_HEAD_DIM)
    svp, svs = shaped(v_bp, v_bs, B_KV_HEADS, B_HEAD_DIM)
    wrows = min(WINDOW, SEQ)
    return (y_prompt, y_sample, dkp, dvp, dip, skp[:, :, SEQ - wrows:], svp[:, :, SEQ - wrows:],
            dks, dvs, dis, sks, svs)
```
